```python
import math
import jax, jax.numpy as jnp
from jax import lax
import numpy as np

D_MODEL = 1024
BATCH = 16
SEQ = 2048
DEPTH = 2

N_META = 16
HEAD_DIM = 64
N_Q_HEADS = 8
N_KV_HEADS = 2
GQA_GROUP = N_Q_HEADS // N_KV_HEADS
ATTN_WIDTH = N_Q_HEADS * HEAD_DIM
KV_WIDTH = N_KV_HEADS * HEAD_DIM
WINDOW = 128
BLOCK = 128
CONV_WIDTH = D_MODEL // 2
CONV_TAPS = 31
N_BRANCHES = 2
REL_BUCKETS = 32
REL_MAX_DIST = 128
D_FF = 2816
FFN_TAPS = 3
LN_EPS = 1e-5
DEEPNORM_ALPHA = (2.0 * DEPTH) ** 0.25
DEEPNORM_BETA = (8.0 * DEPTH) ** -0.25
IN_COLS = ATTN_WIDTH + 2 * KV_WIDTH + 2 * CONV_WIDTH + N_BRANCHES * D_MODEL

kernel_name = "hybrid_conformer_swa_sink_gated"


def layer_norm(x, g, b):
    xf = x.astype(jnp.float32)
    mu = xf.mean(-1, keepdims=True)
    var = jnp.square(xf - mu).mean(-1, keepdims=True)
    return ((xf - mu) * lax.rsqrt(var + LN_EPS) * g + b).astype(x.dtype)


def causal_dwconv(x, w, b):
    taps, c = w.shape
    y = lax.conv_general_dilated(x, w[:, None, :].astype(x.dtype), window_strides=(1,),
                                 padding=[(taps - 1, 0)],
                                 dimension_numbers=('NWC', 'WIO', 'NWC'),
                                 feature_group_count=c)
    return y + b.astype(x.dtype)


def t5_bucket(d):
    n = jnp.maximum(d, 0)
    max_exact = REL_BUCKETS // 2
    nf = jnp.maximum(n, 1).astype(jnp.float32)
    large = max_exact + (jnp.log(nf / max_exact) / math.log(REL_MAX_DIST / max_exact)
                         * (REL_BUCKETS - max_exact)).astype(jnp.int32)
    large = jnp.minimum(large, REL_BUCKETS - 1)
    return jnp.where(n < max_exact, n, large)


def swa_sink_attention(q, k, v, sinks, rel_bias):
    B, T = q.shape[0], q.shape[1]
    pad = BLOCK - N_META
    nblk = (T + pad) // BLOCK
    padw = ((0, 0), (pad, 0), (0, 0), (0, 0))
    qp = jnp.pad(q, padw).reshape(B, nblk, BLOCK, N_KV_HEADS, GQA_GROUP, HEAD_DIM)
    kp = jnp.pad(k, padw).reshape(B, nblk, BLOCK, N_KV_HEADS, HEAD_DIM)
    vp = jnp.pad(v, padw).reshape(B, nblk, BLOCK, N_KV_HEADS, HEAD_DIM)
    shift = ((0, 0), (1, 0), (0, 0), (0, 0), (0, 0))
    kb = jnp.concatenate([jnp.pad(kp[:, :-1], shift), kp], axis=2)
    vb = jnp.concatenate([jnp.pad(vp[:, :-1], shift), vp], axis=2)
    k_meta, v_meta = k[:, :N_META], v[:, :N_META]

    blk = jnp.arange(nblk)[:, None]
    q_pos = blk * BLOCK + jnp.arange(BLOCK)[None, :] - pad
    k_pos = (blk - 1) * BLOCK + jnp.arange(2 * BLOCK)[None, :] - pad
    d_band = q_pos[:, :, None] - k_pos[:, None, :]
    mask_band = (d_band >= 0) & (d_band < WINDOW) & (k_pos >= N_META)[:, None, :]
    d_meta = q_pos[:, :, None] - jnp.arange(N_META)[None, None, :]
    mask_meta = d_meta >= 0

    rb = rel_bias.astype(jnp.float32)
    def bias_of(d):
        bsh = jnp.moveaxis(rb[t5_bucket(d)], -1, 0)
        return bsh.reshape((N_KV_HEADS, GQA_GROUP) + d.shape)

    scale = HEAD_DIM ** -0.5
    neg = jnp.finfo(jnp.float32).min
    s_band = jnp.einsum('bnqhgd,bnkhd->bhgnqk', qp, kb).astype(jnp.float32)
    s_band = jnp.where(mask_band, s_band * scale + bias_of(d_band), neg)
    s_meta = jnp.einsum('bnqhgd,bmhd->bhgnqm', qp, k_meta).astype(jnp.float32)
    s_meta = jnp.where(mask_meta, s_meta * scale + bias_of(d_meta), neg)

    sink = sinks.astype(jnp.float32).reshape(N_KV_HEADS, GQA_GROUP)[None, :, :, None, None, None]
    m = jnp.maximum(jnp.maximum(s_band.max(-1, keepdims=True), s_meta.max(-1, keepdims=True)), sink)
    p_band = jnp.exp(s_band - m)
    p_meta = jnp.exp(s_meta - m)
    denom = p_band.sum(-1, keepdims=True) + p_meta.sum(-1, keepdims=True) + jnp.exp(sink - m)
    p_band = (p_band / denom).astype(v.dtype)
    p_meta = (p_meta / denom).astype(v.dtype)
    o = (jnp.einsum('bhgnqk,bnkhd->bnqhgd', p_band, vb)
         + jnp.einsum('bhgnqm,bmhd->bnqhgd', p_meta, v_meta))
    o = o.reshape(B, nblk * BLOCK, ATTN_WIDTH)
    return o[:, pad:]


def hybrid_layer(h, rel_bias, w_in, b_in, attn_sinks, w_attn_proj, conv_dw, conv_dw_b,
                 conv_ln_g, conv_ln_b, w_conv_proj, w_out, ln1_g, ln1_b,
                 ffn_w_up, ffn_dw, ffn_dw_b, ffn_w_down, ln2_g, ln2_b):
    B, T, _ = h.shape
    z = h @ w_in + b_in
    q, k, v, c_in, gates = jnp.split(
        z, [ATTN_WIDTH, ATTN_WIDTH + KV_WIDTH, ATTN_WIDTH + 2 * KV_WIDTH,
            ATTN_WIDTH + 2 * KV_WIDTH + 2 * CONV_WIDTH], axis=-1)
    a = swa_sink_attention(q.reshape(B, T, N_Q_HEADS, HEAD_DIM),
                           k.reshape(B, T, N_KV_HEADS, HEAD_DIM),
                           v.reshape(B, T, N_KV_HEADS, HEAD_DIM), attn_sinks, rel_bias)
    y_attn = a @ w_attn_proj
    c_val, c_gate = jnp.split(c_in, 2, axis=-1)
    c = c_val * jax.nn.sigmoid(c_gate)
    c = causal_dwconv(c, conv_dw, conv_dw_b)
    c = jax.nn.silu(layer_norm(c, conv_ln_g, conv_ln_b))
    y_conv = c @ w_conv_proj
    g_attn, g_conv = jnp.split(gates, 2, axis=-1)
    mixed = jax.nn.sigmoid(g_attn) * y_attn + jax.nn.sigmoid(g_conv) * y_conv
    h = layer_norm(DEEPNORM_ALPHA * h + mixed @ w_out, ln1_g, ln1_b)
    up = causal_dwconv(h @ ffn_w_up, ffn_dw, ffn_dw_b)
    u, g = jnp.split(up, 2, axis=-1)
    f = (jax.nn.gelu(g, approximate=False) * u) @ ffn_w_down
    return layer_norm(DEEPNORM_ALPHA * h + f, ln2_g, ln2_b)


def setup_inputs(seed: int = 0) -> dict:
    key = jax.random.key(seed)
    ks = jax.random.split(key, 24)
    def nrm(k, shape, s):
        return jax.random.normal(k, shape, jnp.float32) * s
    col_scale = jnp.ones((IN_COLS,), jnp.float32).at[
        ATTN_WIDTH + KV_WIDTH:ATTN_WIDTH + 2 * KV_WIDTH].set(DEEPNORM_BETA)
    return {
        "x": nrm(ks[0], (BATCH, SEQ, D_MODEL), 1.0),
        "meta_tokens": nrm(ks[1], (N_META, D_MODEL), 1.0),
        "in_ln_g": 1.0 + nrm(ks[2], (D_MODEL,), 0.02),
        "in_ln_b": nrm(ks[3], (D_MODEL,), 0.02),
        "rel_bias": nrm(ks[4], (REL_BUCKETS, N_Q_HEADS), 0.1),
        "w_in": nrm(ks[5], (DEPTH, D_MODEL, IN_COLS), D_MODEL ** -0.5) * col_scale,
        "b_in": nrm(ks[6], (DEPTH, IN_COLS), 0.01),
        "attn_sinks": nrm(ks[7], (DEPTH, N_Q_HEADS), 0.5),
        "w_attn_proj": nrm(ks[8], (DEPTH, ATTN_WIDTH, D_MODEL), ATTN_WIDTH ** -0.5 * DEEPNORM_BETA),
        "conv_dw": nrm(ks[9], (DEPTH, CONV_TAPS, CONV_WIDTH), CONV_TAPS ** -0.5),
        "conv_dw_b": nrm(ks[10], (DEPTH, CONV_WIDTH), 0.01),
        "conv_ln_g": 1.0 + nrm(ks[11], (DEPTH, CONV_WIDTH), 0.02),
        "conv_ln_b": nrm(ks[12], (DEPTH, CONV_WIDTH), 0.02),
        "w_conv_proj": nrm(ks[13], (DEPTH, CONV_WIDTH, D_MODEL), CONV_WIDTH ** -0.5 * DEEPNORM_BETA),
        "w_out": nrm(ks[14], (DEPTH, D_MODEL, D_MODEL), D_MODEL ** -0.5 * DEEPNORM_BETA),
        "ln1_g": 1.0 + nrm(ks[15], (DEPTH, D_MODEL), 0.02),
        "ln1_b": nrm(ks[16], (DEPTH, D_MODEL), 0.02),
        "ffn_w_up": nrm(ks[17], (DEPTH, D_MODEL, 2 * D_FF), D_MODEL ** -0.5 * DEEPNORM_BETA),
        "ffn_dw": nrm(ks[18], (DEPTH, FFN_TAPS, 2 * D_FF), FFN_TAPS ** -0.5),
        "ffn_dw_b": nrm(ks[19], (DEPTH, 2 * D_FF), 0.01),
        "ffn_w_down": nrm(ks[20], (DEPTH, D_FF, D_MODEL), D_FF ** -0.5 * DEEPNORM_BETA),
        "ln2_g": 1.0 + nrm(ks[21], (DEPTH, D_MODEL), 0.02),
        "ln2_b": nrm(ks[22], (DEPTH, D_MODEL), 0.02),
    }


def reference(x, meta_tokens, in_ln_g, in_ln_b, rel_bias, w_in, b_in, attn_sinks,
              w_attn_proj, conv_dw, conv_dw_b, conv_ln_g, conv_ln_b, w_conv_proj, w_out,
              ln1_g, ln1_b, ffn_w_up, ffn_dw, ffn_dw_b, ffn_w_down, ln2_g, ln2_b):
    B = x.shape[0]
    meta = jnp.broadcast_to(meta_tokens[None].astype(x.dtype), (B, N_META, D_MODEL))
    h = jnp.concatenate([meta, x], axis=1)
    h = layer_norm(h, in_ln_g, in_ln_b)
    for i in range(DEPTH):
        h = hybrid_layer(h, rel_bias, w_in[i], b_in[i], attn_sinks[i], w_attn_proj[i],
                         conv_dw[i], conv_dw_b[i], conv_ln_g[i], conv_ln_b[i], w_conv_proj[i],
                         w_out[i], ln1_g[i], ln1_b[i], ffn_w_up[i], ffn_dw[i], ffn_dw_b[i],
                         ffn_w_down[i], ln2_g[i], ln2_b[i])
    return h[:, N_META:]
```

```python
import functools
import math

import jax
import jax.numpy as jnp
import numpy as np
from jax import lax
from jax.experimental import pallas as pl
from jax.experimental.pallas import tpu as pltpu

D_MODEL = 1024
DEPTH = 2
N_META = 16
HEAD_DIM = 64
N_Q_HEADS = 8
N_KV_HEADS = 2
GQA_GROUP = N_Q_HEADS // N_KV_HEADS
ATTN_WIDTH = N_Q_HEADS * HEAD_DIM
KV_WIDTH = N_KV_HEADS * HEAD_DIM
WINDOW = 128
BLOCK = 128
CONV_WIDTH = D_MODEL // 2
CONV_TAPS = 31
REL_BUCKETS = 32
REL_MAX_DIST = 128
D_FF = 2816
FFN_TAPS = 3
LN_EPS = 1e-5
DEEPNORM_ALPHA = (2.0 * DEPTH) ** 0.25
IN_COLS = ATTN_WIDTH + 2 * KV_WIDTH + 2 * CONV_WIDTH + 2 * D_MODEL

QKV_END = ATTN_WIDTH + 2 * KV_WIDTH
CIN_END = QKV_END + 2 * CONV_WIDTH

TM = 256
FFN_CHUNK = 256
N_FFN_CHUNKS = D_FF // FFN_CHUNK
CONV_HIST = 32
FFN_HIST = 8
CONV_ROWS = 32
MASKED = -1e30
VMEM_LIMIT_BYTES = 60000 * 1024

BF16 = jnp.bfloat16
F32 = jnp.float32


def _dot(a, b):
    return jnp.dot(a, b, preferred_element_type=F32)


def _dot_nt(a, b):
    return lax.dot_general(a, b, (((1,), (1,)), ((), ())), preferred_element_type=F32)


def _layer_norm(x, g, b):
    mu = jnp.mean(x, axis=-1, keepdims=True)
    xc = x - mu
    var = jnp.mean(xc * xc, axis=-1, keepdims=True)
    return xc * lax.rsqrt(var + LN_EPS) * g + b


def _gelu(x):
    return 0.5 * x * (1.0 + lax.erf(x * math.sqrt(0.5)))


def _in_proj(h, w_in, b_in):
    hb = h.astype(BF16)
    qkv = _dot(hb, w_in[:, 0:QKV_END]) + b_in[:, 0:QKV_END]
    cin = _dot(hb, w_in[:, QKV_END:CIN_END]) + b_in[:, QKV_END:CIN_END]
    c = cin[:, :CONV_WIDTH] * jax.nn.sigmoid(cin[:, CONV_WIDTH:])
    return hb, qkv, c


def _attend(q_blk, band_kv, meta_kv, band_bias, meta_bias, sinks):
    outs = []
    for a in range(N_Q_HEADS):
        kvh = a // GQA_GROUP
        ks = slice(kvh * HEAD_DIM, (kvh + 1) * HEAD_DIM)
        vs = slice(KV_WIDTH + kvh * HEAD_DIM, KV_WIDTH + (kvh + 1) * HEAD_DIM)
        q_a = q_blk[:, a * HEAD_DIM:(a + 1) * HEAD_DIM]
        sink = sinks[a]
        s_m = _dot_nt(q_a, meta_kv[:, ks]) + meta_bias(a)
        m = jnp.maximum(jnp.max(s_m, axis=-1, keepdims=True), sink)
        if band_kv is not None:
            s_b = _dot_nt(q_a, band_kv[:, ks]) + band_bias(a)
            m = jnp.maximum(m, jnp.max(s_b, axis=-1, keepdims=True))
        p_m = jnp.exp(s_m - m)
        den = jnp.sum(p_m, axis=-1, keepdims=True) + jnp.exp(sink - m)
        o = _dot(p_m.astype(BF16), meta_kv[:, vs])
        if band_kv is not None:
            p_b = jnp.exp(s_b - m)
            den = den + jnp.sum(p_b, axis=-1, keepdims=True)
            o = o + _dot(p_b.astype(BF16), band_kv[:, vs])
        outs.append(o / den)
    return jnp.concatenate(outs, axis=1)


def _conv31(cbuf, m_rows, cdw, cdwb):
    rows = min(m_rows, CONV_ROWS)
    first = CONV_HIST - (CONV_TAPS - 1)
    chunks = []
    for r0 in range(0, m_rows, rows):
        acc = jnp.broadcast_to(cdwb[...], (rows, CONV_WIDTH))
        for k in range(CONV_TAPS):
            acc = acc + cdw[k:k + 1, :] * cbuf[r0 + first + k:r0 + first + k + rows, :]
        chunks.append(acc)
    return chunks[0] if len(chunks) == 1 else jnp.concatenate(chunks, axis=0)


def _conv3(buf, m_rows, w, b):
    first = FFN_HIST - (FFN_TAPS - 1)
    acc = b
    for k in range(FFN_TAPS):
        acc = acc + w[k:k + 1, :] * buf[first + k:first + k + m_rows, :]
    return acc


def _mix_and_ffn(m_rows, h, hb, attn_b, conv_b, w_in, b_in, w_ap, w_cp, w_out, ln1g, ln1b,
                 w_up, fdw, fdwb, w_down, ln2g, ln2b, upu, upg, upcarry, up_out):
    y_attn = _dot(attn_b, w_ap[...])
    y_conv = _dot(conv_b, w_cp[...])
    gates = _dot(hb, w_in[:, CIN_END:IN_COLS]) + b_in[:, CIN_END:IN_COLS]
    mixed = (jax.nn.sigmoid(gates[:, :D_MODEL]) * y_attn
             + jax.nn.sigmoid(gates[:, D_MODEL:]) * y_conv)
    h1 = _layer_norm(DEEPNORM_ALPHA * h + _dot(mixed.astype(BF16), w_out[...]),
                     ln1g[...], ln1b[...])
    h1b = h1.astype(BF16)

    acc = jnp.zeros((m_rows, D_MODEL), F32)
    for j in range(N_FFN_CHUNKS):
        ucols = slice(j * FFN_CHUNK, (j + 1) * FFN_CHUNK)
        gcols = slice(D_FF + j * FFN_CHUNK, D_FF + (j + 1) * FFN_CHUNK)
        u = _dot(h1b, w_up[:, ucols])
        g = _dot(h1b, w_up[:, gcols])
        if up_out is not None:
            up_out[:, ucols] = u
            up_out[:, gcols] = g
        upu[0:FFN_HIST, :] = upcarry[j, 0]
        upg[0:FFN_HIST, :] = upcarry[j, 1]
        upu[FFN_HIST:FFN_HIST + m_rows, :] = u
        upg[FFN_HIST:FFN_HIST + m_rows, :] = g
        upcarry[j, 0] = upu[m_rows:m_rows + FFN_HIST, :]
        upcarry[j, 1] = upg[m_rows:m_rows + FFN_HIST, :]
        cu = _conv3(upu, m_rows, fdw[:, ucols], fdwb[:, ucols])
        cg = _conv3(upg, m_rows, fdw[:, gcols], fdwb[:, gcols])
        f = _gelu(cg) * cu
        acc = acc + _dot(f.astype(BF16), w_down[j * FFN_CHUNK:(j + 1) * FFN_CHUNK, :])
    return _layer_norm(DEEPNORM_ALPHA * h1 + acc, ln2g[...], ln2b[...])


def _main_kernel(apply_in_ln,
                 h_ref, ilg, ilb, w_in, b_in, sinks, bband, bmeta, kvmeta, cmeta, upmeta,
                 w_ap, cdw, cdwb, clng, clnb, w_cp, w_out, ln1g, ln1b,
                 w_up, fdw, fdwb, w_down, ln2g, ln2b,
                 out_ref,
                 q_buf, kv_buf, cbuf, a_buf, upu, upg, upcarry):
    t = pl.program_id(1)

    @pl.when(t == 0)
    def _():
        kv_buf[0:BLOCK, :] = jnp.zeros((BLOCK, 2 * KV_WIDTH), BF16)
        cbuf[0:CONV_HIST - N_META, :] = jnp.zeros((CONV_HIST - N_META, CONV_WIDTH), F32)
        cbuf[CONV_HIST - N_META:CONV_HIST, :] = cmeta[...]
        for j in range(N_FFN_CHUNKS):
            upcarry[j, 0] = upmeta[N_META - FFN_HIST:N_META, j * FFN_CHUNK:(j + 1) * FFN_CHUNK]
            upcarry[j, 1] = upmeta[N_META - FFN_HIST:N_META,
                                   D_FF + j * FFN_CHUNK:D_FF + (j + 1) * FFN_CHUNK]

    h = h_ref[0]
    if apply_in_ln:
        h = _layer_norm(h, ilg[...], ilb[...])
    hb, qkv, c = _in_proj(h, w_in, b_in)
    q_buf[...] = (qkv[:, :ATTN_WIDTH] * (HEAD_DIM ** -0.5)).astype(BF16)
    kv_buf[BLOCK:BLOCK + TM, :] = qkv[:, ATTN_WIDTH:QKV_END].astype(BF16)
    cbuf[CONV_HIST:CONV_HIST + TM, :] = c

    first = (t == 0).astype(jnp.int32)
    for j in range(TM // BLOCK):
        r0 = j * BLOCK
        bidx = first if j == 0 else 0
        o = _attend(q_buf[r0:r0 + BLOCK, :], kv_buf[r0:r0 + 2 * BLOCK, :], kvmeta[...],
                    lambda a, bidx=bidx: bband[bidx, a], lambda a, bidx=bidx: bmeta[bidx, a],
                    sinks)
        a_buf[r0:r0 + BLOCK, :] = o.astype(BF16)

    cc = _conv31(cbuf, TM, cdw, cdwb)
    conv_b = jax.nn.silu(_layer_norm(cc, clng[...], clnb[...])).astype(BF16)

    out_ref[0] = _mix_and_ffn(TM, h, hb, a_buf[...], conv_b, w_in, b_in, w_ap, w_cp, w_out,
                              ln1g, ln1b, w_up, fdw, fdwb, w_down, ln2g, ln2b,
                              upu, upg, upcarry, None)

    kv_buf[0:BLOCK, :] = kv_buf[TM:TM + BLOCK, :]
    cbuf[0:CONV_HIST, :] = cbuf[TM:TM + CONV_HIST, :]


def _meta_kernel(apply_in_ln,
                 h_ref, ilg, ilb, w_in, b_in, sinks, bmetaq,
                 w_ap, cdw, cdwb, clng, clnb, w_cp, w_out, ln1g, ln1b,
                 w_up, fdw, fdwb, w_down, ln2g, ln2b,
                 hout_ref, kvpad_ref, cmeta_ref, upmeta_ref,
                 cbuf, upu, upg, upcarry):
    h = h_ref[...]
    if apply_in_ln:
        h = _layer_norm(h, ilg[...], ilb[...])
    hb, qkv, c = _in_proj(h, w_in, b_in)
    q = (qkv[:, :ATTN_WIDTH] * (HEAD_DIM ** -0.5)).astype(BF16)
    kvpad_ref[...] = jnp.zeros((BLOCK, 2 * KV_WIDTH), BF16)
    kvpad_ref[0:N_META, :] = qkv[:, ATTN_WIDTH:QKV_END].astype(BF16)
    cmeta_ref[...] = c
    cbuf[0:CONV_HIST, :] = jnp.zeros((CONV_HIST, CONV_WIDTH), F32)
    cbuf[CONV_HIST:CONV_HIST + N_META, :] = c
    upcarry[...] = jnp.zeros(upcarry.shape, F32)

    o = _attend(q, None, kvpad_ref[...], None, lambda a: bmetaq[a], sinks)
    cc = _conv31(cbuf, N_META, cdw, cdwb)
    conv_b = jax.nn.silu(_layer_norm(cc, clng[...], clnb[...])).astype(BF16)
    hout_ref[...] = _mix_and_ffn(N_META, h, hb, o.astype(BF16), conv_b, w_in, b_in, w_ap, w_cp,
                                 w_out, ln1g, ln1b, w_up, fdw, fdwb, w_down, ln2g, ln2b,
                                 upu, upg, upcarry, upmeta_ref)


def _t5_bucket(d):
    n = jnp.maximum(d, 0)
    max_exact = REL_BUCKETS // 2
    nf = jnp.maximum(n, 1).astype(F32)
    large = max_exact + (jnp.log(nf / max_exact) / math.log(REL_MAX_DIST / max_exact)
                         * (REL_BUCKETS - max_exact)).astype(jnp.int32)
    large = jnp.minimum(large, REL_BUCKETS - 1)
    return jnp.where(n < max_exact, n, large)


def _bias_tables(rel_bias):
    rb = rel_bias.astype(F32)

    def bias_of(d, valid):
        b = jnp.moveaxis(rb[_t5_bucket(d)], -1, 0)
        return jnp.where(valid[None], b, MASKED)

    i = jnp.arange(BLOCK)[:, None]
    j = jnp.arange(2 * BLOCK)[None, :]
    d = i + BLOCK - j
    valid = (d >= 0) & (d < WINDOW)
    band = jnp.stack([bias_of(d, valid), bias_of(d, valid & (j >= BLOCK))])
    far = N_META + BLOCK - (N_META - 1)
    assert 16 + int(math.log(far / 16) / math.log(REL_MAX_DIST / 16) * 16) >= REL_BUCKETS - 1
    mcol = jnp.arange(BLOCK)[None, :]
    mvalid = jnp.broadcast_to(mcol < N_META, (BLOCK, BLOCK))
    meta = jnp.stack([bias_of(N_META + BLOCK + i - mcol, mvalid),
                      bias_of(N_META + i - mcol, mvalid)])
    qi = jnp.arange(N_META)[:, None]
    metaq = bias_of(qi - mcol, (mcol <= qi) & (mcol < N_META))
    return band, meta, metaq


def _vmem_spec():
    return pl.BlockSpec(memory_space=pltpu.VMEM)


def _meta_layer(apply_in_ln, hm, ilg, ilb, lw, bmetaq):
    n_up = 2 * D_FF
    out_shape = (jax.ShapeDtypeStruct((N_META, D_MODEL), F32),
                 jax.ShapeDtypeStruct((BLOCK, 2 * KV_WIDTH), BF16),
                 jax.ShapeDtypeStruct((N_META, CONV_WIDTH), F32),
                 jax.ShapeDtypeStruct((N_META, n_up), F32))
    ins = (hm, ilg, ilb, lw["w_in"], lw["b_in"], lw["sinks"], bmetaq,
           lw["w_ap"], lw["cdw"], lw["cdwb"], lw["clng"], lw["clnb"], lw["w_cp"], lw["w_out"],
           lw["ln1g"], lw["ln1b"], lw["w_up"], lw["fdw"], lw["fdwb"], lw["w_down"],
           lw["ln2g"], lw["ln2b"])
    in_specs = [_vmem_spec() for _ in ins]
    in_specs[5] = pl.BlockSpec(memory_space=pltpu.SMEM)
    return pl.pallas_call(
        functools.partial(_meta_kernel, apply_in_ln),
        out_shape=out_shape,
        in_specs=in_specs,
        out_specs=tuple(_vmem_spec() for _ in out_shape),
        scratch_shapes=[
            pltpu.VMEM((CONV_HIST + N_META, CONV_WIDTH), F32),
            pltpu.VMEM((FFN_HIST + N_META, FFN_CHUNK), F32),
            pltpu.VMEM((FFN_HIST + N_META, FFN_CHUNK), F32),
            pltpu.VMEM((N_FFN_CHUNKS, 2, FFN_HIST, FFN_CHUNK), F32),
        ],
        compiler_params=pltpu.CompilerParams(vmem_limit_bytes=VMEM_LIMIT_BYTES),
        name="meta_layer",
    )(*ins)


def _main_layer(apply_in_ln, h, ilg, ilb, lw, bband, bmeta, kvmeta, cmeta, upmeta):
    batch, seq, _ = h.shape
    assert seq % TM == 0 and TM % BLOCK == 0
    ins = (h, ilg, ilb, lw["w_in"], lw["b_in"], lw["sinks"], bband, bmeta, kvmeta, cmeta, upmeta,
           lw["w_ap"], lw["cdw"], lw["cdwb"], lw["clng"], lw["clnb"], lw["w_cp"], lw["w_out"],
           lw["ln1g"], lw["ln1b"], lw["w_up"], lw["fdw"], lw["fdwb"], lw["w_down"],
           lw["ln2g"], lw["ln2b"])
    tile = pl.BlockSpec((1, TM, D_MODEL), lambda b, t: (b, t, 0))
    in_specs = [_vmem_spec() for _ in ins]
    in_specs[0] = tile
    in_specs[5] = pl.BlockSpec(memory_space=pltpu.SMEM)
    return pl.pallas_call(
        functools.partial(_main_kernel, apply_in_ln),
        out_shape=jax.ShapeDtypeStruct((batch, seq, D_MODEL), F32),
        grid=(batch, seq // TM),
        in_specs=in_specs,
        out_specs=tile,
        scratch_shapes=[
            pltpu.VMEM((TM, ATTN_WIDTH), BF16),
            pltpu.VMEM((BLOCK + TM, 2 * KV_WIDTH), BF16),
            pltpu.VMEM((CONV_HIST + TM, CONV_WIDTH), F32),
            pltpu.VMEM((TM, ATTN_WIDTH), BF16),
            pltpu.VMEM((FFN_HIST + TM, FFN_CHUNK), F32),
            pltpu.VMEM((FFN_HIST + TM, FFN_CHUNK), F32),
            pltpu.VMEM((N_FFN_CHUNKS, 2, FFN_HIST, FFN_CHUNK), F32),
        ],
        compiler_params=pltpu.CompilerParams(
            dimension_semantics=("arbitrary", "arbitrary"),
            vmem_limit_bytes=VMEM_LIMIT_BYTES),
        name="main_layer",
    )(*ins)


def _layer_weights(i, w_in, b_in, attn_sinks, w_attn_proj, conv_dw, conv_dw_b, conv_ln_g,
                   conv_ln_b, w_conv_proj, w_out, ln1_g, ln1_b, ffn_w_up, ffn_dw, ffn_dw_b,
                   ffn_w_down, ln2_g, ln2_b):
    row = lambda v: v[i][None, :].astype(F32)
    return dict(
        w_in=w_in[i].astype(BF16), b_in=row(b_in), sinks=attn_sinks[i].astype(F32),
        w_ap=w_attn_proj[i].astype(BF16), cdw=conv_dw[i].astype(F32), cdwb=row(conv_dw_b),
        clng=row(conv_ln_g), clnb=row(conv_ln_b), w_cp=w_conv_proj[i].astype(BF16),
        w_out=w_out[i].astype(BF16), ln1g=row(ln1_g), ln1b=row(ln1_b),
        w_up=ffn_w_up[i].astype(BF16), fdw=ffn_dw[i].astype(F32), fdwb=row(ffn_dw_b),
        w_down=ffn_w_down[i].astype(BF16), ln2g=row(ln2_g), ln2b=row(ln2_b))


def kernel(x, meta_tokens, in_ln_g, in_ln_b, rel_bias, w_in, b_in, attn_sinks, w_attn_proj, conv_dw, conv_dw_b, conv_ln_g, conv_ln_b, w_conv_proj, w_out, ln1_g, ln1_b, ffn_w_up, ffn_dw, ffn_dw_b, ffn_w_down, ln2_g, ln2_b):
    bband, bmeta, bmetaq = _bias_tables(rel_bias)
    ilg = in_ln_g[None, :].astype(F32)
    ilb = in_ln_b[None, :].astype(F32)
    h = x.astype(F32)
    hm = meta_tokens.astype(F32)
    for i in range(DEPTH):
        lw = _layer_weights(i, w_in, b_in, attn_sinks, w_attn_proj, conv_dw, conv_dw_b,
                            conv_ln_g, conv_ln_b, w_conv_proj, w_out, ln1_g, ln1_b, ffn_w_up,
                            ffn_dw, ffn_dw_b, ffn_w_down, ln2_g, ln2_b)
        hm_next, kvmeta, cmeta, upmeta = _meta_layer(i == 0, hm, ilg, ilb, lw, bmetaq)
        h = _main_layer(i == 0, h, ilg, ilb, lw, bband, bmeta, kvmeta, cmeta, upmeta)
        hm = hm_next
    return h
```

```python
import functools
import math

import jax
import jax.numpy as jnp
from jax import lax
from jax.experimental import pallas as pl
from jax.experimental.pallas import tpu as pltpu

D_MODEL = 1024
DEPTH = 2
N_META = 16
HEAD_DIM = 64
N_Q_HEADS = 8
N_KV_HEADS = 2
GQA_GROUP = N_Q_HEADS // N_KV_HEADS
ATTN_WIDTH = N_Q_HEADS * HEAD_DIM
KV_WIDTH = N_KV_HEADS * HEAD_DIM
GROUP_WIDTH = GQA_GROUP * HEAD_DIM
WINDOW = 128
BLOCK = 128
CONV_WIDTH = D_MODEL // 2
CONV_TAPS = 31
REL_BUCKETS = 32
REL_MAX_DIST = 128
D_FF = 2816
FFN_TAPS = 3
LN_EPS = 1e-5
DEEPNORM_ALPHA = (2.0 * DEPTH) ** 0.25
IN_COLS = ATTN_WIDTH + 2 * KV_WIDTH + 2 * CONV_WIDTH + 2 * D_MODEL

QKV_END = ATTN_WIDTH + 2 * KV_WIDTH
CIN_END = QKV_END + 2 * CONV_WIDTH

LANES = 128
TM = 256
FFN_CHUNK = 256
N_FFN_CHUNKS = D_FF // FFN_CHUNK
FFN_SLABS = 2 * FFN_CHUNK // LANES
FFN_BUFS = 3
CONV_SLABS = CONV_WIDTH // LANES
CONV_HIST = 32
FFN_HIST = 8
CONV_ROWS = 64
MASKED = -1e30
VMEM_LIMIT_BYTES = 60000 * 1024

BF16 = jnp.bfloat16
F32 = jnp.float32


def _dot(a, b):
    return jnp.dot(a, b, preferred_element_type=F32)


def _dot_nt(a, b):
    return lax.dot_general(a, b, (((1,), (1,)), ((), ())), preferred_element_type=F32)


def _layer_norm(x, g, b):
    mu = jnp.mean(x, axis=-1, keepdims=True)
    xc = x - mu
    var = jnp.mean(xc * xc, axis=-1, keepdims=True)
    return xc * lax.rsqrt(var + LN_EPS) * g + b


def _gelu(x):
    return 0.5 * x * (1.0 + lax.erf(x * math.sqrt(0.5)))


def _tok(start, n):
    return pl.ds(2 * start, n, stride=2)


def _in_proj(h, w_in, b_in):
    hb = h.astype(BF16)
    qkv = _dot(hb, w_in[:, 0:QKV_END]) + b_in[:, 0:QKV_END]
    cin = _dot(hb, w_in[:, QKV_END:CIN_END]) + b_in[:, QKV_END:CIN_END]
    c = cin[:, :CONV_WIDTH] * jax.nn.sigmoid(cin[:, CONV_WIDTH:])
    return hb, qkv, c


def _attn_scores(q_blk, kvh, band_kv, meta_kv):
    heads = range(kvh * GQA_GROUP, (kvh + 1) * GQA_GROUP)
    ks = slice(kvh * HEAD_DIM, (kvh + 1) * HEAD_DIM)
    q = jnp.concatenate([q_blk[:, a * HEAD_DIM:(a + 1) * HEAD_DIM] for a in heads], axis=0)
    s_m = _dot_nt(q, meta_kv[:, ks])
    s_b = None if band_kv is None else _dot_nt(q, band_kv[:, ks])
    return s_m, s_b


def _attn_finish(scores, rows, kvh, band_kv, meta_kv, band_bias, meta_bias, sinks):
    s_m, s_b = scores
    heads = range(kvh * GQA_GROUP, (kvh + 1) * GQA_GROUP)
    vs = slice(KV_WIDTH + kvh * HEAD_DIM, KV_WIDTH + (kvh + 1) * HEAD_DIM)
    sink = jnp.concatenate([jnp.full((rows, 1), sinks[a], F32) for a in heads], axis=0)
    s_m = s_m + meta_bias
    m = jnp.maximum(jnp.max(s_m, axis=-1, keepdims=True), sink)
    if band_kv is not None:
        s_b = s_b + band_bias
        m = jnp.maximum(m, jnp.max(s_b, axis=-1, keepdims=True))
    p_m = jnp.exp(s_m - m)
    den = jnp.sum(p_m, axis=-1, keepdims=True) + jnp.exp(sink - m)
    o = _dot(p_m.astype(BF16), meta_kv[:, vs])
    if band_kv is not None:
        p_b = jnp.exp(s_b - m)
        den = den + jnp.sum(p_b, axis=-1, keepdims=True)
        o = o + _dot(p_b.astype(BF16), band_kv[:, vs])
    o = o / den
    return jnp.concatenate([o[g * rows:(g + 1) * rows] for g in range(GQA_GROUP)], axis=1)


def _conv31_rows(cb, r0, rows, cdw, cdwb):
    first = CONV_HIST - (CONV_TAPS - 1)
    slabs = []
    for s in range(CONV_SLABS):
        ls = slice(s * LANES, (s + 1) * LANES)
        acc = jnp.broadcast_to(cdwb[:, ls], (rows, LANES))
        for k in range(CONV_TAPS):
            acc = acc + cdw[k:k + 1, ls] * cb[s, _tok(r0 + first + k, rows), :]
        slabs.append(acc)
    return jnp.concatenate(slabs, axis=1)


def _conv_branch_rows(cb, r0, rows, cdw, cdwb, clng, clnb):
    cc = _conv31_rows(cb, r0, rows, cdw, cdwb)
    return jax.nn.silu(_layer_norm(cc, clng[...], clnb[...])).astype(BF16)


def _conv3_slab(ub, buf, slab, m_rows, w, b):
    first = FFN_HIST - (FFN_TAPS - 1)
    acc = b
    for k in range(FFN_TAPS):
        acc = acc + w[k:k + 1, :] * ub[buf, slab, _tok(first + k, m_rows), :]
    return acc


def _gate(hb, w_in, b_in, lo):
    return jax.nn.sigmoid(_dot(hb, w_in[:, lo:lo + D_MODEL]) + b_in[:, lo:lo + D_MODEL])


def _mix_out(h, gate_attn, y_attn, gate_conv, y_conv, w_out, ln1g, ln1b):
    mixed = gate_attn * y_attn + gate_conv * y_conv
    return _layer_norm(DEEPNORM_ALPHA * h + _dot(mixed.astype(BF16), w_out[...]),
                       ln1g[...], ln1b[...])


def _ffn(m_rows, h1, w_up, fdw, fdwb, w_down, ln2g, ln2b, ub, upcarry, up_out):
    h1b = h1.astype(BF16)
    half = FFN_SLABS // 2
    ahead = FFN_BUFS - 1

    def chunk_cols(j, s):
        base = j * FFN_CHUNK + (s % half) * LANES + (D_FF if s >= half else 0)
        return slice(base, base + LANES)

    def up_proj(j):
        return (_dot(h1b, w_up[:, j * FFN_CHUNK:(j + 1) * FFN_CHUNK]),
                _dot(h1b, w_up[:, D_FF + j * FFN_CHUNK:D_FF + (j + 1) * FFN_CHUNK]))

    acc = jnp.zeros((m_rows, D_MODEL), F32)
    ups = {j: up_proj(j) for j in range(ahead)}
    for j in range(N_FFN_CHUNKS):
        u, g = ups.pop(j)
        buf = j % FFN_BUFS
        if up_out is not None:
            up_out[:, j * FFN_CHUNK:(j + 1) * FFN_CHUNK] = u
            up_out[:, D_FF + j * FFN_CHUNK:D_FF + (j + 1) * FFN_CHUNK] = g
        for s in range(FFN_SLABS):
            src = u if s < half else g
            ls = slice((s % half) * LANES, (s % half + 1) * LANES)
            ub[buf, s, _tok(0, FFN_HIST), :] = upcarry[j, s]
            ub[buf, s, _tok(FFN_HIST, m_rows), :] = src[:, ls]
            upcarry[j, s] = ub[buf, s, _tok(m_rows, FFN_HIST), :]
        if j + ahead < N_FFN_CHUNKS:
            ups[j + ahead] = up_proj(j + ahead)
        conv = [_conv3_slab(ub, buf, s, m_rows, fdw[:, chunk_cols(j, s)], fdwb[:, chunk_cols(j, s)])
                for s in range(FFN_SLABS)]
        f = jnp.concatenate([_gelu(conv[half + s]) * conv[s] for s in range(half)], axis=1)
        acc = acc + _dot(f.astype(BF16), w_down[j * FFN_CHUNK:(j + 1) * FFN_CHUNK, :])
    return _layer_norm(DEEPNORM_ALPHA * h1 + acc, ln2g[...], ln2b[...])


def _main_kernel(apply_in_ln,
                 h_ref, ilg, ilb, w_in, b_in, sinks, bband, bmeta, kvmeta, cmeta, upmeta,
                 w_ap, cdw, cdwb, clng, clnb, w_cp, w_out, ln1g, ln1b,
                 w_up, fdw, fdwb, w_down, ln2g, ln2b,
                 out_ref,
                 q_buf, kv_buf, cb, a_buf, ub, upcarry):
    t = pl.program_id(1)

    @pl.when(t == 0)
    def _():
        kv_buf[0:BLOCK, :] = jnp.zeros((BLOCK, 2 * KV_WIDTH), BF16)
        for s in range(CONV_SLABS):
            cb[s, _tok(0, CONV_HIST - N_META), :] = jnp.zeros((CONV_HIST - N_META, LANES), F32)
            cb[s, _tok(CONV_HIST - N_META, N_META), :] = cmeta[:, s * LANES:(s + 1) * LANES]
        half = FFN_SLABS // 2
        for j in range(N_FFN_CHUNKS):
            for s in range(FFN_SLABS):
                base = j * FFN_CHUNK + (s % half) * LANES + (D_FF if s >= half else 0)
                upcarry[j, s] = upmeta[N_META - FFN_HIST:N_META, base:base + LANES]

    h = h_ref[0]
    if apply_in_ln:
        h = _layer_norm(h, ilg[...], ilb[...])
    hb, qkv, c = _in_proj(h, w_in, b_in)
    q_buf[...] = (qkv[:, :ATTN_WIDTH] * (HEAD_DIM ** -0.5)).astype(BF16)
    kv_buf[BLOCK:BLOCK + TM, :] = qkv[:, ATTN_WIDTH:QKV_END].astype(BF16)
    for s in range(CONV_SLABS):
        cb[s, _tok(CONV_HIST, TM), :] = c[:, s * LANES:(s + 1) * LANES]

    first = (t == 0).astype(jnp.int32)
    meta_kv = kvmeta[...]

    def scores(j, kvh):
        r0 = j * BLOCK
        return _attn_scores(q_buf[r0:r0 + BLOCK, :], kvh, kv_buf[r0:r0 + 2 * BLOCK, :], meta_kv)

    def finish(j, kvh, sc):
        r0 = j * BLOCK
        bidx = first if j == 0 else 0
        heads = pl.ds(kvh * GQA_GROUP, GQA_GROUP)
        band_bias = bband[bidx, heads].reshape(GQA_GROUP * BLOCK, 2 * BLOCK)
        meta_bias = bmeta[bidx, heads].reshape(GQA_GROUP * BLOCK, BLOCK)
        o = _attn_finish(sc, BLOCK, kvh, kv_buf[r0:r0 + 2 * BLOCK, :], meta_kv,
                         band_bias, meta_bias, sinks)
        a_buf[r0:r0 + BLOCK, kvh * GROUP_WIDTH:(kvh + 1) * GROUP_WIDTH] = o.astype(BF16)

    def conv_proj(r0):
        cvals = [_conv_branch_rows(cb, r, CONV_ROWS, cdw, cdwb, clng, clnb)
                 for r in range(r0, r0 + BLOCK, CONV_ROWS)]
        return _dot(jnp.concatenate(cvals, axis=0), w_cp[...])

    units = [(j, kvh) for j in range(TM // BLOCK) for kvh in range(N_KV_HEADS)]
    assert len(units) == 4
    sc0 = scores(*units[0])
    sc1 = scores(*units[1])
    gate_attn = _gate(hb, w_in, b_in, CIN_END)
    finish(*units[0], sc0)
    sc2 = scores(*units[2])
    finish(*units[1], sc1)
    sc3 = scores(*units[3])
    y_conv_top = conv_proj(0)
    gate_conv = _gate(hb, w_in, b_in, CIN_END + D_MODEL)
    finish(*units[2], sc2)
    finish(*units[3], sc3)
    y_conv_bot = conv_proj(BLOCK)
    y_attn = _dot(a_buf[...], w_ap[...])
    y_conv = jnp.concatenate([y_conv_top, y_conv_bot], axis=0)
    h1 = _mix_out(h, gate_attn, y_attn, gate_conv, y_conv, w_out, ln1g, ln1b)
    out_ref[0] = _ffn(TM, h1, w_up, fdw, fdwb, w_down, ln2g, ln2b, ub, upcarry, None)

    kv_buf[0:BLOCK, :] = kv_buf[TM:TM + BLOCK, :]
    for s in range(CONV_SLABS):
        cb[s, _tok(0, CONV_HIST), :] = cb[s, _tok(TM, CONV_HIST), :]


def _meta_kernel(apply_in_ln,
                 h_ref, ilg, ilb, w_in, b_in, sinks, bmetaq,
                 w_ap, cdw, cdwb, clng, clnb, w_cp, w_out, ln1g, ln1b,
                 w_up, fdw, fdwb, w_down, ln2g, ln2b,
                 hout_ref, kvpad_ref, cmeta_ref, upmeta_ref,
                 cb, ub, upcarry):
    h = h_ref[...]
    if apply_in_ln:
        h = _layer_norm(h, ilg[...], ilb[...])
    hb, qkv, c = _in_proj(h, w_in, b_in)
    q = (qkv[:, :ATTN_WIDTH] * (HEAD_DIM ** -0.5)).astype(BF16)
    kvpad_ref[...] = jnp.zeros((BLOCK, 2 * KV_WIDTH), BF16)
    kvpad_ref[0:N_META, :] = qkv[:, ATTN_WIDTH:QKV_END].astype(BF16)
    cmeta_ref[...] = c
    for s in range(CONV_SLABS):
        cb[s, _tok(0, CONV_HIST), :] = jnp.zeros((CONV_HIST, LANES), F32)
        cb[s, _tok(CONV_HIST, N_META), :] = c[:, s * LANES:(s + 1) * LANES]
    upcarry[...] = jnp.zeros(upcarry.shape, F32)

    meta_kv = kvpad_ref[...]
    o = jnp.concatenate(
        [_attn_finish(_attn_scores(q, kvh, None, meta_kv), N_META, kvh, None, meta_kv, None,
                      bmetaq[kvh * GQA_GROUP:(kvh + 1) * GQA_GROUP].reshape(GQA_GROUP * N_META, BLOCK),
                      sinks)
         for kvh in range(N_KV_HEADS)], axis=1)
    conv_b = _conv_branch_rows(cb, 0, N_META, cdw, cdwb, clng, clnb)
    y_attn = _dot(o.astype(BF16), w_ap[...])
    y_conv = _dot(conv_b, w_cp[...])
    h1 = _mix_out(h, _gate(hb, w_in, b_in, CIN_END), y_attn,
                  _gate(hb, w_in, b_in, CIN_END + D_MODEL), y_conv, w_out, ln1g, ln1b)
    hout_ref[...] = _ffn(N_META, h1, w_up, fdw, fdwb, w_down, ln2g, ln2b, ub, upcarry, upmeta_ref)


def _t5_bucket(d):
    n = jnp.maximum(d, 0)
    max_exact = REL_BUCKETS // 2
    nf = jnp.maximum(n, 1).astype(F32)
    large = max_exact + (jnp.log(nf / max_exact) / math.log(REL_MAX_DIST / max_exact)
                         * (REL_BUCKETS - max_exact)).astype(jnp.int32)
    large = jnp.minimum(large, REL_BUCKETS - 1)
    return jnp.where(n < max_exact, n, large)


def _bias_tables(rel_bias):
    rb = rel_bias.astype(F32)

    def bias_of(d, valid):
        onehot = jax.nn.one_hot(_t5_bucket(d), REL_BUCKETS, dtype=F32)
        b = jnp.einsum("qkb,bh->hqk", onehot, rb, precision=lax.Precision.HIGHEST)
        return jnp.where(valid[None], b, MASKED)

    i = jnp.arange(BLOCK)[:, None]
    j = jnp.arange(2 * BLOCK)[None, :]
    d = i + BLOCK - j
    valid = (d >= 0) & (d < WINDOW)
    band = jnp.stack([bias_of(d, valid), bias_of(d, valid & (j >= BLOCK))])
    far = N_META + BLOCK - (N_META - 1)
    assert 16 + int(math.log(far / 16) / math.log(REL_MAX_DIST / 16) * 16) >= REL_BUCKETS - 1
    mcol = jnp.arange(BLOCK)[None, :]
    mvalid = jnp.broadcast_to(mcol < N_META, (BLOCK, BLOCK))
    meta = jnp.stack([bias_of(N_META + BLOCK + i - mcol, mvalid),
                      bias_of(N_META + i - mcol, mvalid)])
    qi = jnp.arange(N_META)[:, None]
    metaq = bias_of(qi - mcol, (mcol <= qi) & (mcol < N_META))
    return band, meta, metaq


def _vmem_spec():
    return pl.BlockSpec(memory_space=pltpu.VMEM)


def _conv_scratch(m_rows):
    return [
        pltpu.VMEM((CONV_SLABS, 2 * (CONV_HIST + m_rows), LANES), F32),
        pltpu.VMEM((FFN_BUFS, FFN_SLABS, 2 * (FFN_HIST + m_rows), LANES), F32),
        pltpu.VMEM((N_FFN_CHUNKS, FFN_SLABS, FFN_HIST, LANES), F32),
    ]


def _meta_layer(apply_in_ln, hm, ilg, ilb, lw, bmetaq):
    n_up = 2 * D_FF
    out_shape = (jax.ShapeDtypeStruct((N_META, D_MODEL), F32),
                 jax.ShapeDtypeStruct((BLOCK, 2 * KV_WIDTH), BF16),
                 jax.ShapeDtypeStruct((N_META, CONV_WIDTH), F32),
                 jax.ShapeDtypeStruct((N_META, n_up), F32))
    ins = (hm, ilg, ilb, lw["w_in"], lw["b_in"], lw["sinks"], bmetaq,
           lw["w_ap"], lw["cdw"], lw["cdwb"], lw["clng"], lw["clnb"], lw["w_cp"], lw["w_out"],
           lw["ln1g"], lw["ln1b"], lw["w_up"], lw["fdw"], lw["fdwb"], lw["w_down"],
           lw["ln2g"], lw["ln2b"])
    in_specs = [_vmem_spec() for _ in ins]
    in_specs[5] = pl.BlockSpec(memory_space=pltpu.SMEM)
    return pl.pallas_call(
        functools.partial(_meta_kernel, apply_in_ln),
        out_shape=out_shape,
        in_specs=in_specs,
        out_specs=tuple(_vmem_spec() for _ in out_shape),
        scratch_shapes=_conv_scratch(N_META),
        compiler_params=pltpu.CompilerParams(vmem_limit_bytes=VMEM_LIMIT_BYTES),
        name="meta_layer",
    )(*ins)


def _main_layer(apply_in_ln, h, ilg, ilb, lw, bband, bmeta, kvmeta, cmeta, upmeta):
    batch, seq, _ = h.shape
    assert seq % TM == 0 and TM % BLOCK == 0 and TM % CONV_ROWS == 0
    ins = (h, ilg, ilb, lw["w_in"], lw["b_in"], lw["sinks"], bband, bmeta, kvmeta, cmeta, upmeta,
           lw["w_ap"], lw["cdw"], lw["cdwb"], lw["clng"], lw["clnb"], lw["w_cp"], lw["w_out"],
           lw["ln1g"], lw["ln1b"], lw["w_up"], lw["fdw"], lw["fdwb"], lw["w_down"],
           lw["ln2g"], lw["ln2b"])
    tile = pl.BlockSpec((1, TM, D_MODEL), lambda b, t: (b, t, 0))
    in_specs = [_vmem_spec() for _ in ins]
    in_specs[0] = tile
    in_specs[5] = pl.BlockSpec(memory_space=pltpu.SMEM)
    return pl.pallas_call(
        functools.partial(_main_kernel, apply_in_ln),
        out_shape=jax.ShapeDtypeStruct((batch, seq, D_MODEL), F32),
        grid=(batch, seq // TM),
        in_specs=in_specs,
        out_specs=tile,
        scratch_shapes=[
            pltpu.VMEM((TM, ATTN_WIDTH), BF16),
            pltpu.VMEM((BLOCK + TM, 2 * KV_WIDTH), BF16),
        ] + _conv_scratch(TM)[:1] + [
            pltpu.VMEM((TM, ATTN_WIDTH), BF16),
        ] + _conv_scratch(TM)[1:],
        compiler_params=pltpu.CompilerParams(
            dimension_semantics=("arbitrary", "arbitrary"),
            vmem_limit_bytes=VMEM_LIMIT_BYTES),
        name="main_layer",
    )(*ins)


def _layer_weights(i, w_in, b_in, attn_sinks, w_attn_proj, conv_dw, conv_dw_b, conv_ln_g,
                   conv_ln_b, w_conv_proj, w_out, ln1_g, ln1_b, ffn_w_up, ffn_dw, ffn_dw_b,
                   ffn_w_down, ln2_g, ln2_b):
    row = lambda v: v[i][None, :].astype(F32)
    return dict(
        w_in=w_in[i].astype(BF16), b_in=row(b_in), sinks=attn_sinks[i].astype(F32),
        w_ap=w_attn_proj[i].astype(BF16), cdw=conv_dw[i].astype(F32), cdwb=row(conv_dw_b),
        clng=row(conv_ln_g), clnb=row(conv_ln_b), w_cp=w_conv_proj[i].astype(BF16),
        w_out=w_out[i].astype(BF16), ln1g=row(ln1_g), ln1b=row(ln1_b),
        w_up=ffn_w_up[i].astype(BF16), fdw=ffn_dw[i].astype(F32), fdwb=row(ffn_dw_b),
        w_down=ffn_w_down[i].astype(BF16), ln2g=row(ln2_g), ln2b=row(ln2_b))


def kernel(x, meta_tokens, in_ln_g, in_ln_b, rel_bias, w_in, b_in, attn_sinks, w_attn_proj, conv_dw, conv_dw_b, conv_ln_g, conv_ln_b, w_conv_proj, w_out, ln1_g, ln1_b, ffn_w_up, ffn_dw, ffn_dw_b, ffn_w_down, ln2_g, ln2_b):
    bband, bmeta, bmetaq = _bias_tables(rel_bias)
    ilg = in_ln_g[None, :].astype(F32)
    ilb = in_ln_b[None, :].astype(F32)
    h = x.astype(F32)
    hm = meta_tokens.astype(F32)
    for i in range(DEPTH):
        lw = _layer_weights(i, w_in, b_in, attn_sinks, w_attn_proj, conv_dw, conv_dw_b,
                            conv_ln_g, conv_ln_b, w_conv_proj, w_out, ln1_g, ln1_b, ffn_w_up,
                            ffn_dw, ffn_dw_b, ffn_w_down, ln2_g, ln2_b)
        hm_next, kvmeta, cmeta, upmeta = _meta_layer(i == 0, hm, ilg, ilb, lw, bmetaq)
        h = _main_layer(i == 0, h, ilg, ilb, lw, bband, bmeta, kvmeta, cmeta, upmeta)
        hm = hm_next
    return h
```

```python
import functools
import itertools
import math

import jax
import jax.numpy as jnp
from jax import lax
from jax.experimental import pallas as pl
from jax.experimental.pallas import tpu as pltpu

D_MODEL = 1024
DEPTH = 2
N_META = 16
HEAD_DIM = 64
N_Q_HEADS = 8
N_KV_HEADS = 2
GQA_GROUP = N_Q_HEADS // N_KV_HEADS
ATTN_WIDTH = N_Q_HEADS * HEAD_DIM
KV_WIDTH = N_KV_HEADS * HEAD_DIM
GROUP_WIDTH = GQA_GROUP * HEAD_DIM
WINDOW = 128
BLOCK = 128
CONV_WIDTH = D_MODEL // 2
CONV_TAPS = 31
REL_BUCKETS = 32
REL_MAX_DIST = 128
D_FF = 2816
FFN_TAPS = 3
LN_EPS = 1e-5
DEEPNORM_ALPHA = (2.0 * DEPTH) ** 0.25
IN_COLS = ATTN_WIDTH + 2 * KV_WIDTH + 2 * CONV_WIDTH + 2 * D_MODEL

QKV_END = ATTN_WIDTH + 2 * KV_WIDTH
CIN_END = QKV_END + 2 * CONV_WIDTH

LANES = 128
TM = 256
FFN_CHUNK = 256
N_FFN_CHUNKS = D_FF // FFN_CHUNK
FFN_SLABS = 2 * FFN_CHUNK // LANES
FFN_BUFS = 3
CONV_SLABS = CONV_WIDTH // LANES
CONV_HIST = 32
FFN_HIST = 8
CONV_ROWS = 64
MASKED = -1e30
VMEM_LIMIT_BYTES = 60000 * 1024

BF16 = jnp.bfloat16
F32 = jnp.float32


def _dot(a, b):
    return jnp.dot(a, b, preferred_element_type=F32)


def _dot_nt(a, b):
    return lax.dot_general(a, b, (((1,), (1,)), ((), ())), preferred_element_type=F32)


def _layer_norm(x, g, b):
    mu = jnp.mean(x, axis=-1, keepdims=True)
    xc = x - mu
    var = jnp.mean(xc * xc, axis=-1, keepdims=True)
    return xc * lax.rsqrt(var + LN_EPS) * g + b


def _gelu(x):
    return 0.5 * x * (1.0 + lax.erf(x * math.sqrt(0.5)))


def _tok(start, n):
    return pl.ds(2 * start, n, stride=2)


def _in_proj(h, w_in, b_in):
    hb = h.astype(BF16)
    qkv = _dot(hb, w_in[:, 0:QKV_END]) + b_in[:, 0:QKV_END]
    cin = _dot(hb, w_in[:, QKV_END:CIN_END]) + b_in[:, QKV_END:CIN_END]
    c = cin[:, :CONV_WIDTH] * jax.nn.sigmoid(cin[:, CONV_WIDTH:])
    return hb, qkv, c


def _attn_scores(q_blk, kvh, band_kv, meta_kv):
    heads = range(kvh * GQA_GROUP, (kvh + 1) * GQA_GROUP)
    ks = slice(kvh * HEAD_DIM, (kvh + 1) * HEAD_DIM)
    q = jnp.concatenate([q_blk[:, a * HEAD_DIM:(a + 1) * HEAD_DIM] for a in heads], axis=0)
    s_m = _dot_nt(q, meta_kv[:, ks])
    s_b = None if band_kv is None else _dot_nt(q, band_kv[:, ks])
    return s_m, s_b


def _attn_finish(scores, rows, kvh, band_kv, meta_kv, band_bias, meta_bias, sinks):
    s_m, s_b = scores
    heads = range(kvh * GQA_GROUP, (kvh + 1) * GQA_GROUP)
    vs = slice(KV_WIDTH + kvh * HEAD_DIM, KV_WIDTH + (kvh + 1) * HEAD_DIM)
    sink = jnp.concatenate([jnp.full((rows, 1), sinks[a], F32) for a in heads], axis=0)
    s_m = s_m + meta_bias
    m = jnp.maximum(jnp.max(s_m, axis=-1, keepdims=True), sink)
    if band_kv is not None:
        s_b = s_b + band_bias
        m = jnp.maximum(m, jnp.max(s_b, axis=-1, keepdims=True))
    p_m = jnp.exp(s_m - m)
    den = jnp.sum(p_m, axis=-1, keepdims=True) + jnp.exp(sink - m)
    o = _dot(p_m.astype(BF16), meta_kv[:, vs])
    if band_kv is not None:
        p_b = jnp.exp(s_b - m)
        den = den + jnp.sum(p_b, axis=-1, keepdims=True)
        o = o + _dot(p_b.astype(BF16), band_kv[:, vs])
    o = o / den
    return jnp.concatenate([o[g * rows:(g + 1) * rows] for g in range(GQA_GROUP)], axis=1)


def _conv31_rows(cb, r0, rows, cdw, cdwb):
    first = CONV_HIST - (CONV_TAPS - 1)
    slabs = []
    for s in range(CONV_SLABS):
        ls = slice(s * LANES, (s + 1) * LANES)
        acc = jnp.broadcast_to(cdwb[:, ls], (rows, LANES))
        for k in range(CONV_TAPS):
            acc = acc + cdw[k:k + 1, ls] * cb[s, _tok(r0 + first + k, rows), :]
        slabs.append(acc)
    return jnp.concatenate(slabs, axis=1)


def _conv_branch_rows(cb, r0, rows, cdw, cdwb, clng, clnb):
    cc = _conv31_rows(cb, r0, rows, cdw, cdwb)
    return jax.nn.silu(_layer_norm(cc, clng[...], clnb[...])).astype(BF16)


def _conv3_slab(ub, buf, slab, m_rows, w, b):
    first = FFN_HIST - (FFN_TAPS - 1)
    acc = b
    for k in range(FFN_TAPS):
        acc = acc + w[k:k + 1, :] * ub[buf, slab, _tok(first + k, m_rows), :]
    return acc


def _gate(hb, w_in, b_in, lo):
    return jax.nn.sigmoid(_dot(hb, w_in[:, lo:lo + D_MODEL]) + b_in[:, lo:lo + D_MODEL])


def _mix_out(h, gate_attn, y_attn, gate_conv, y_conv, w_out, ln1g, ln1b):
    mixed = (gate_attn * y_attn + gate_conv * y_conv).astype(BF16)
    return _layer_norm(DEEPNORM_ALPHA * h + _dot(mixed, w_out[...]), ln1g[...], ln1b[...])


def _ffn(*args):
    result = []
    for _ in _ffn_stages(*args, result.append):
        pass
    return result[0]


def _ffn_stages(m_rows, h1, w_up, fdw, fdwb, w_down, ln2g, ln2b, ub, upcarry, up_out, emit):
    h1b = h1.astype(BF16)
    half = FFN_SLABS // 2
    ahead = FFN_BUFS - 1

    def chunk_cols(j, s):
        base = j * FFN_CHUNK + (s % half) * LANES + (D_FF if s >= half else 0)
        return slice(base, base + LANES)

    def up_proj(j):
        return (_dot(h1b, w_up[:, j * FFN_CHUNK:(j + 1) * FFN_CHUNK]),
                _dot(h1b, w_up[:, D_FF + j * FFN_CHUNK:D_FF + (j + 1) * FFN_CHUNK]))

    acc = jnp.zeros((m_rows, D_MODEL), F32)
    ups = {j: up_proj(j) for j in range(ahead)}
    yield
    for j in range(N_FFN_CHUNKS):
        u, g = ups.pop(j)
        buf = j % FFN_BUFS
        if up_out is not None:
            up_out[:, j * FFN_CHUNK:(j + 1) * FFN_CHUNK] = u
            up_out[:, D_FF + j * FFN_CHUNK:D_FF + (j + 1) * FFN_CHUNK] = g
        for s in range(FFN_SLABS):
            src = u if s < half else g
            ls = slice((s % half) * LANES, (s % half + 1) * LANES)
            ub[buf, s, _tok(0, FFN_HIST), :] = upcarry[j, s]
            ub[buf, s, _tok(FFN_HIST, m_rows), :] = src[:, ls]
            upcarry[j, s] = ub[buf, s, _tok(m_rows, FFN_HIST), :]
        if j + ahead < N_FFN_CHUNKS:
            ups[j + ahead] = up_proj(j + ahead)
        conv = [_conv3_slab(ub, buf, s, m_rows, fdw[:, chunk_cols(j, s)], fdwb[:, chunk_cols(j, s)])
                for s in range(FFN_SLABS)]
        f = jnp.concatenate([_gelu(conv[half + s]) * conv[s] for s in range(half)], axis=1)
        acc = acc + _dot(f.astype(BF16), w_down[j * FFN_CHUNK:(j + 1) * FFN_CHUNK, :])
        yield
    emit(_layer_norm(DEEPNORM_ALPHA * h1 + acc, ln2g[...], ln2b[...]))


def _main_kernel(apply_in_ln, n_tiles, tiles_per_seq,
                 h_ref, ilg, ilb, w_in, b_in, sinks, bband, bmeta, kvmeta, cmeta, upmeta,
                 w_ap, cdw, cdwb, clng, clnb, w_cp, w_out, ln1g, ln1b,
                 w_up, fdw, fdwb, w_down, ln2g, ln2b,
                 out_ref,
                 q_buf, kv_seq, cb, a_buf, h1s, ub, upcarry):
    n = pl.program_id(0)
    t_front = jnp.minimum(n, n_tiles - 1) % tiles_per_seq
    t_back = jnp.maximum(n - 1, 0) % tiles_per_seq
    wslot = n % 2
    rslot = 1 - wslot

    @pl.when(n == 0)
    def _():
        h1s[rslot] = jnp.zeros((TM, D_MODEL), F32)

    @pl.when(t_front == 0)
    def _():
        kv_seq[0:BLOCK, :] = jnp.zeros((BLOCK, 2 * KV_WIDTH), BF16)
        for s in range(CONV_SLABS):
            cb[s, _tok(0, CONV_HIST - N_META), :] = jnp.zeros((CONV_HIST - N_META, LANES), F32)
            cb[s, _tok(CONV_HIST - N_META, N_META), :] = cmeta[:, s * LANES:(s + 1) * LANES]

    @pl.when(t_back == 0)
    def _():
        half = FFN_SLABS // 2
        for j in range(N_FFN_CHUNKS):
            for s in range(FFN_SLABS):
                base = j * FFN_CHUNK + (s % half) * LANES + (D_FF if s >= half else 0)
                upcarry[j, s] = upmeta[N_META - FFN_HIST:N_META, base:base + LANES]

    first = (t_front == 0).astype(jnp.int32)
    meta_kv = kvmeta[...]
    kv_row0 = pl.multiple_of(t_front * TM, TM)

    def band(j):
        return kv_seq[pl.ds(kv_row0 + j * BLOCK, 2 * BLOCK), :]

    def scores(j, kvh):
        r0 = j * BLOCK
        return _attn_scores(q_buf[r0:r0 + BLOCK, :], kvh, band(j), meta_kv)

    def finish(j, kvh, sc):
        r0 = j * BLOCK
        bidx = first if j == 0 else 0
        heads = pl.ds(kvh * GQA_GROUP, GQA_GROUP)
        band_bias = bband[bidx, heads].reshape(GQA_GROUP * BLOCK, 2 * BLOCK)
        meta_bias = bmeta[bidx, heads].reshape(GQA_GROUP * BLOCK, BLOCK)
        o = _attn_finish(sc, BLOCK, kvh, band(j), meta_kv, band_bias, meta_bias, sinks)
        a_buf[r0:r0 + BLOCK, kvh * GROUP_WIDTH:(kvh + 1) * GROUP_WIDTH] = o.astype(BF16)

    def conv_proj(r0):
        cvals = [_conv_branch_rows(cb, r, CONV_ROWS, cdw, cdwb, clng, clnb)
                 for r in range(r0, r0 + BLOCK, CONV_ROWS)]
        return _dot(jnp.concatenate(cvals, axis=0), w_cp[...])

    def mixer_half():
        h = h_ref[0]
        if apply_in_ln:
            h = _layer_norm(h, ilg[...], ilb[...])
        hb, qkv, c = _in_proj(h, w_in, b_in)
        q_buf[...] = (qkv[:, :ATTN_WIDTH] * (HEAD_DIM ** -0.5)).astype(BF16)
        kv_seq[pl.ds(kv_row0 + BLOCK, TM), :] = qkv[:, ATTN_WIDTH:QKV_END].astype(BF16)
        for s in range(CONV_SLABS):
            cb[s, _tok(CONV_HIST, TM), :] = c[:, s * LANES:(s + 1) * LANES]
        yield
        units = [(j, kvh) for j in range(TM // BLOCK) for kvh in range(N_KV_HEADS)]
        assert len(units) == 4
        sc0 = scores(*units[0])
        sc1 = scores(*units[1])
        yield
        gate_attn = _gate(hb, w_in, b_in, CIN_END)
        yield
        finish(*units[0], sc0)
        yield
        sc2 = scores(*units[2])
        finish(*units[1], sc1)
        yield
        sc3 = scores(*units[3])
        y_conv_top = conv_proj(0)
        yield
        gate_conv = _gate(hb, w_in, b_in, CIN_END + D_MODEL)
        yield
        finish(*units[2], sc2)
        finish(*units[3], sc3)
        yield
        y_conv_bot = conv_proj(BLOCK)
        y_attn = _dot(a_buf[...], w_ap[...])
        yield
        y_conv = jnp.concatenate([y_conv_top, y_conv_bot], axis=0)
        h1s[wslot] = _mix_out(h, gate_attn, y_attn, gate_conv, y_conv, w_out, ln1g, ln1b)

    def store_out(y):
        out_ref[0] = y

    ffn_half = _ffn_stages(TM, h1s[rslot], w_up, fdw, fdwb, w_down, ln2g, ln2b, ub, upcarry,
                           None, store_out)
    for _ in itertools.zip_longest(mixer_half(), ffn_half):
        pass

    for s in range(CONV_SLABS):
        cb[s, _tok(0, CONV_HIST), :] = cb[s, _tok(TM, CONV_HIST), :]


def _meta_kernel(apply_in_ln,
                 h_ref, ilg, ilb, w_in, b_in, sinks, bmetaq,
                 w_ap, cdw, cdwb, clng, clnb, w_cp, w_out, ln1g, ln1b,
                 w_up, fdw, fdwb, w_down, ln2g, ln2b,
                 hout_ref, kvpad_ref, cmeta_ref, upmeta_ref,
                 cb, ub, upcarry):
    h = h_ref[...]
    if apply_in_ln:
        h = _layer_norm(h, ilg[...], ilb[...])
    hb, qkv, c = _in_proj(h, w_in, b_in)
    q = (qkv[:, :ATTN_WIDTH] * (HEAD_DIM ** -0.5)).astype(BF16)
    kvpad_ref[...] = jnp.zeros((BLOCK, 2 * KV_WIDTH), BF16)
    kvpad_ref[0:N_META, :] = qkv[:, ATTN_WIDTH:QKV_END].astype(BF16)
    cmeta_ref[...] = c
    for s in range(CONV_SLABS):
        cb[s, _tok(0, CONV_HIST), :] = jnp.zeros((CONV_HIST, LANES), F32)
        cb[s, _tok(CONV_HIST, N_META), :] = c[:, s * LANES:(s + 1) * LANES]
    upcarry[...] = jnp.zeros(upcarry.shape, F32)

    meta_kv = kvpad_ref[...]
    o = jnp.concatenate(
        [_attn_finish(_attn_scores(q, kvh, None, meta_kv), N_META, kvh, None, meta_kv, None,
                      bmetaq[kvh * GQA_GROUP:(kvh + 1) * GQA_GROUP].reshape(GQA_GROUP * N_META, BLOCK),
                      sinks)
         for kvh in range(N_KV_HEADS)], axis=1)
    conv_b = _conv_branch_rows(cb, 0, N_META, cdw, cdwb, clng, clnb)
    y_attn = _dot(o.astype(BF16), w_ap[...])
    y_conv = _dot(conv_b, w_cp[...])
    h1 = _mix_out(h, _gate(hb, w_in, b_in, CIN_END), y_attn,
                  _gate(hb, w_in, b_in, CIN_END + D_MODEL), y_conv, w_out, ln1g, ln1b)
    hout_ref[...] = _ffn(N_META, h1, w_up, fdw, fdwb, w_down, ln2g, ln2b, ub, upcarry, upmeta_ref)


def _t5_bucket(d):
    n = jnp.maximum(d, 0)
    max_exact = REL_BUCKETS // 2
    nf = jnp.maximum(n, 1).astype(F32)
    large = max_exact + (jnp.log(nf / max_exact) / math.log(REL_MAX_DIST / max_exact)
                         * (REL_BUCKETS - max_exact)).astype(jnp.int32)
    large = jnp.minimum(large, REL_BUCKETS - 1)
    return jnp.where(n < max_exact, n, large)


def _bias_tables(rel_bias):
    rb = rel_bias.astype(F32)

    def bias_of(d, valid):
        onehot = jax.nn.one_hot(_t5_bucket(d), REL_BUCKETS, dtype=F32)
        b = jnp.einsum("qkb,bh->hqk", onehot, rb, precision=lax.Precision.HIGHEST)
        return jnp.where(valid[None], b, MASKED)

    i = jnp.arange(BLOCK)[:, None]
    j = jnp.arange(2 * BLOCK)[None, :]
    d = i + BLOCK - j
    valid = (d >= 0) & (d < WINDOW)
    band = jnp.stack([bias_of(d, valid), bias_of(d, valid & (j >= BLOCK))])
    far = N_META + BLOCK - (N_META - 1)
    assert 16 + int(math.log(far / 16) / math.log(REL_MAX_DIST / 16) * 16) >= REL_BUCKETS - 1
    mcol = jnp.arange(BLOCK)[None, :]
    mvalid = jnp.broadcast_to(mcol < N_META, (BLOCK, BLOCK))
    meta = jnp.stack([bias_of(N_META + BLOCK + i - mcol, mvalid),
                      bias_of(N_META + i - mcol, mvalid)])
    qi = jnp.arange(N_META)[:, None]
    metaq = bias_of(qi - mcol, (mcol <= qi) & (mcol < N_META))
    return band, meta, metaq


def _vmem_spec():
    return pl.BlockSpec(memory_space=pltpu.VMEM)


def _conv_scratch(m_rows):
    return [
        pltpu.VMEM((CONV_SLABS, 2 * (CONV_HIST + m_rows), LANES), F32),
        pltpu.VMEM((FFN_BUFS, FFN_SLABS, 2 * (FFN_HIST + m_rows), LANES), F32),
        pltpu.VMEM((N_FFN_CHUNKS, FFN_SLABS, FFN_HIST, LANES), F32),
    ]


def _meta_layer(apply_in_ln, hm, ilg, ilb, lw, bmetaq):
    n_up = 2 * D_FF
    out_shape = (jax.ShapeDtypeStruct((N_META, D_MODEL), F32),
                 jax.ShapeDtypeStruct((BLOCK, 2 * KV_WIDTH), BF16),
                 jax.ShapeDtypeStruct((N_META, CONV_WIDTH), F32),
                 jax.ShapeDtypeStruct((N_META, n_up), F32))
    ins = (hm, ilg, ilb, lw["w_in"], lw["b_in"], lw["sinks"], bmetaq,
           lw["w_ap"], lw["cdw"], lw["cdwb"], lw["clng"], lw["clnb"], lw["w_cp"], lw["w_out"],
           lw["ln1g"], lw["ln1b"], lw["w_up"], lw["fdw"], lw["fdwb"], lw["w_down"],
           lw["ln2g"], lw["ln2b"])
    in_specs = [_vmem_spec() for _ in ins]
    in_specs[5] = pl.BlockSpec(memory_space=pltpu.SMEM)
    return pl.pallas_call(
        functools.partial(_meta_kernel, apply_in_ln),
        out_shape=out_shape,
        in_specs=in_specs,
        out_specs=tuple(_vmem_spec() for _ in out_shape),
        scratch_shapes=_conv_scratch(N_META),
        compiler_params=pltpu.CompilerParams(vmem_limit_bytes=VMEM_LIMIT_BYTES),
        name="meta_layer",
    )(*ins)


def _main_layer(apply_in_ln, h, ilg, ilb, lw, bband, bmeta, kvmeta, cmeta, upmeta):
    batch, seq, _ = h.shape
    assert seq % TM == 0 and TM % BLOCK == 0 and TM % CONV_ROWS == 0
    ins = (h, ilg, ilb, lw["w_in"], lw["b_in"], lw["sinks"], bband, bmeta, kvmeta, cmeta, upmeta,
           lw["w_ap"], lw["cdw"], lw["cdwb"], lw["clng"], lw["clnb"], lw["w_cp"], lw["w_out"],
           lw["ln1g"], lw["ln1b"], lw["w_up"], lw["fdw"], lw["fdwb"], lw["w_down"],
           lw["ln2g"], lw["ln2b"])
    tiles_per_seq = seq // TM
    n_tiles = batch * tiles_per_seq

    def tile_of(n):
        return (n // tiles_per_seq, n % tiles_per_seq, 0)

    in_tile = pl.BlockSpec((1, TM, D_MODEL), lambda n: tile_of(jnp.minimum(n, n_tiles - 1)))
    out_tile = pl.BlockSpec((1, TM, D_MODEL), lambda n: tile_of(jnp.maximum(n - 1, 0)))
    in_specs = [_vmem_spec() for _ in ins]
    in_specs[0] = in_tile
    in_specs[5] = pl.BlockSpec(memory_space=pltpu.SMEM)
    return pl.pallas_call(
        functools.partial(_main_kernel, apply_in_ln, n_tiles, tiles_per_seq),
        out_shape=jax.ShapeDtypeStruct((batch, seq, D_MODEL), F32),
        grid=(n_tiles + 1,),
        in_specs=in_specs,
        out_specs=out_tile,
        scratch_shapes=[
            pltpu.VMEM((TM, ATTN_WIDTH), BF16),
            pltpu.VMEM((BLOCK + seq, 2 * KV_WIDTH), BF16),
        ] + _conv_scratch(TM)[:1] + [
            pltpu.VMEM((TM, ATTN_WIDTH), BF16),
            pltpu.VMEM((2, TM, D_MODEL), F32),
        ] + _conv_scratch(TM)[1:],
        compiler_params=pltpu.CompilerParams(
            dimension_semantics=("arbitrary",),
            vmem_limit_bytes=VMEM_LIMIT_BYTES),
        name="main_layer",
    )(*ins)


def _layer_weights(i, w_in, b_in, attn_sinks, w_attn_proj, conv_dw, conv_dw_b, conv_ln_g,
                   conv_ln_b, w_conv_proj, w_out, ln1_g, ln1_b, ffn_w_up, ffn_dw, ffn_dw_b,
                   ffn_w_down, ln2_g, ln2_b):
    row = lambda v: v[i][None, :].astype(F32)
    return dict(
        w_in=w_in[i].astype(BF16), b_in=row(b_in), sinks=attn_sinks[i].astype(F32),
        w_ap=w_attn_proj[i].astype(BF16), cdw=conv_dw[i].astype(F32), cdwb=row(conv_dw_b),
        clng=row(conv_ln_g), clnb=row(conv_ln_b), w_cp=w_conv_proj[i].astype(BF16),
        w_out=w_out[i].astype(BF16), ln1g=row(ln1_g), ln1b=row(ln1_b),
        w_up=ffn_w_up[i].astype(BF16), fdw=ffn_dw[i].astype(F32), fdwb=row(ffn_dw_b),
        w_down=ffn_w_down[i].astype(BF16), ln2g=row(ln2_g), ln2b=row(ln2_b))


def kernel(x, meta_tokens, in_ln_g, in_ln_b, rel_bias, w_in, b_in, attn_sinks, w_attn_proj, conv_dw, conv_dw_b, conv_ln_g, conv_ln_b, w_conv_proj, w_out, ln1_g, ln1_b, ffn_w_up, ffn_dw, ffn_dw_b, ffn_w_down, ln2_g, ln2_b):
    bband, bmeta, bmetaq = _bias_tables(rel_bias)
    ilg = in_ln_g[None, :].astype(F32)
    ilb = in_ln_b[None, :].astype(F32)
    h = x.astype(F32)
    hm = meta_tokens.astype(F32)
    for i in range(DEPTH):
        lw = _layer_weights(i, w_in, b_in, attn_sinks, w_attn_proj, conv_dw, conv_dw_b,
                            conv_ln_g, conv_ln_b, w_conv_proj, w_out, ln1_g, ln1_b, ffn_w_up,
                            ffn_dw, ffn_dw_b, ffn_w_down, ln2_g, ln2_b)
        hm_next, kvmeta, cmeta, upmeta = _meta_layer(i == 0, hm, ilg, ilb, lw, bmetaq)
        h = _main_layer(i == 0, h, ilg, ilb, lw, bband, bmeta, kvmeta, cmeta, upmeta)
        hm = hm_next
    return h
```

```python
import functools
import math

import jax
import jax.numpy as jnp
from jax import lax
from jax.experimental import pallas as pl
from jax.experimental.pallas import tpu as pltpu

D_MODEL = 1024
DEPTH = 2
N_META = 16
HEAD_DIM = 64
N_Q_HEADS = 8
N_KV_HEADS = 2
GQA_GROUP = N_Q_HEADS // N_KV_HEADS
ATTN_WIDTH = N_Q_HEADS * HEAD_DIM
KV_WIDTH = N_KV_HEADS * HEAD_DIM
GROUP_WIDTH = GQA_GROUP * HEAD_DIM
WINDOW = 128
BLOCK = 128
CONV_WIDTH = D_MODEL // 2
CONV_TAPS = 31
REL_BUCKETS = 32
REL_MAX_DIST = 128
D_FF = 2816
FFN_TAPS = 3
LN_EPS = 1e-5
DEEPNORM_ALPHA = (2.0 * DEPTH) ** 0.25
IN_COLS = ATTN_WIDTH + 2 * KV_WIDTH + 2 * CONV_WIDTH + 2 * D_MODEL

QKV_END = ATTN_WIDTH + 2 * KV_WIDTH
CIN_END = QKV_END + 2 * CONV_WIDTH

LANES = 128
TM = 256
FFN_CHUNK = 256
N_FFN_CHUNKS = D_FF // FFN_CHUNK
FFN_SLABS = 2 * FFN_CHUNK // LANES
FFN_BUFS = 3
CONV_SLABS = CONV_WIDTH // LANES
CONV_HIST = 32
FFN_HIST = 8
CONV_ROWS = 64
MASKED = -1e30
VMEM_LIMIT_BYTES = 60000 * 1024

BF16 = jnp.bfloat16
F32 = jnp.float32


def _dot(a, b):
    return jnp.dot(a, b, preferred_element_type=F32)


def _dot_nt(a, b):
    return lax.dot_general(a, b, (((1,), (1,)), ((), ())), preferred_element_type=F32)


def _layer_norm(x, g, b):
    mu = jnp.mean(x, axis=-1, keepdims=True)
    xc = x - mu
    var = jnp.mean(xc * xc, axis=-1, keepdims=True)
    return xc * lax.rsqrt(var + LN_EPS) * g + b


def _gelu(x):
    return 0.5 * x * (1.0 + lax.erf(x * math.sqrt(0.5)))


def _tok(start, n):
    return pl.ds(2 * start, n, stride=2)


def _in_proj(h, w_in, b_in):
    hb = h.astype(BF16)
    qkv = _dot(hb, w_in[:, 0:QKV_END]) + b_in[:, 0:QKV_END]
    cin = _dot(hb, w_in[:, QKV_END:CIN_END]) + b_in[:, QKV_END:CIN_END]
    c = cin[:, :CONV_WIDTH] * jax.nn.sigmoid(cin[:, CONV_WIDTH:])
    return hb, qkv, c


def _attn_scores(q_blk, kvh, band_kv, meta_kv):
    heads = range(kvh * GQA_GROUP, (kvh + 1) * GQA_GROUP)
    ks = slice(kvh * HEAD_DIM, (kvh + 1) * HEAD_DIM)
    q = jnp.concatenate([q_blk[:, a * HEAD_DIM:(a + 1) * HEAD_DIM] for a in heads], axis=0)
    s_m = _dot_nt(q, meta_kv[:, ks])
    s_b = None if band_kv is None else _dot_nt(q, band_kv[:, ks])
    return s_m, s_b


def _attn_finish(scores, rows, kvh, band_kv, meta_kv, band_bias, meta_bias, sinks):
    s_m, s_b = scores
    heads = range(kvh * GQA_GROUP, (kvh + 1) * GQA_GROUP)
    vs = slice(KV_WIDTH + kvh * HEAD_DIM, KV_WIDTH + (kvh + 1) * HEAD_DIM)
    sink = jnp.concatenate([jnp.full((rows, 1), sinks[a], F32) for a in heads], axis=0)
    s_m = s_m + meta_bias
    m = jnp.maximum(jnp.max(s_m, axis=-1, keepdims=True), sink)
    if band_kv is not None:
        s_b = s_b + band_bias
        m = jnp.maximum(m, jnp.max(s_b, axis=-1, keepdims=True))
    p_m = jnp.exp(s_m - m)
    den = jnp.sum(p_m, axis=-1, keepdims=True) + jnp.exp(sink - m)
    o = _dot(p_m.astype(BF16), meta_kv[:, vs])
    if band_kv is not None:
        p_b = jnp.exp(s_b - m)
        den = den + jnp.sum(p_b, axis=-1, keepdims=True)
        o = o + _dot(p_b.astype(BF16), band_kv[:, vs])
    o = o / den
    return jnp.concatenate([o[g * rows:(g + 1) * rows] for g in range(GQA_GROUP)], axis=1)


def _attn_scores_t(q_blk, kvh, k_band, k_meta):
    heads = range(kvh * GQA_GROUP, (kvh + 1) * GQA_GROUP)
    ks = slice(kvh * HEAD_DIM, (kvh + 1) * HEAD_DIM)
    q = jnp.concatenate([q_blk[:, a * HEAD_DIM:(a + 1) * HEAD_DIM] for a in heads], axis=0)
    k_all = jnp.concatenate([k_band[:, ks], k_meta[:, ks]], axis=0)
    return _dot_nt(k_all, q)


def _attn_finish_t(s_t, kvh, vt_band, vt_meta, band_bias, meta_bias, sink):
    vrows = slice(kvh * HEAD_DIM, (kvh + 1) * HEAD_DIM)
    cols = s_t.shape[1]
    s_b = s_t[:2 * BLOCK] + band_bias
    s_m = s_t[2 * BLOCK:] + meta_bias
    m = jnp.maximum(jnp.maximum(jnp.max(s_b, axis=0, keepdims=True),
                                jnp.max(s_m, axis=0, keepdims=True)), sink)
    p_b = jnp.exp(s_b - m)
    p_m = jnp.exp(s_m - m)
    den = (jnp.sum(p_b, axis=0, keepdims=True) + jnp.sum(p_m, axis=0, keepdims=True)
           + jnp.exp(sink - m))
    p_m_pad = jnp.concatenate([p_m.astype(BF16), jnp.zeros((BLOCK - N_META, cols), BF16)], axis=0)
    o_t = (_dot(vt_band[vrows], p_b.astype(BF16)) + _dot(vt_meta[vrows], p_m_pad)) / den
    stacked = jnp.concatenate([o_t[:, g * BLOCK:(g + 1) * BLOCK] for g in range(GQA_GROUP)], axis=0)
    return stacked.T


def _conv31_rows(cb, r0, rows, cdw, cdwb):
    first = CONV_HIST - (CONV_TAPS - 1)
    slabs = []
    for s in range(CONV_SLABS):
        ls = slice(s * LANES, (s + 1) * LANES)
        acc = jnp.broadcast_to(cdwb[:, ls], (rows, LANES))
        for k in range(CONV_TAPS):
            acc = acc + cdw[k:k + 1, ls] * cb[s, _tok(r0 + first + k, rows), :]
        slabs.append(acc)
    return jnp.concatenate(slabs, axis=1)


def _conv_branch_rows(cb, r0, rows, cdw, cdwb, clng, clnb):
    cc = _conv31_rows(cb, r0, rows, cdw, cdwb)
    return jax.nn.silu(_layer_norm(cc, clng[...], clnb[...])).astype(BF16)


def _conv3_slab(ub, buf, slab, m_rows, w, b):
    first = FFN_HIST - (FFN_TAPS - 1)
    acc = b
    for k in range(FFN_TAPS):
        acc = acc + w[k:k + 1, :] * ub[buf, slab, _tok(first + k, m_rows), :]
    return acc


def _gate(hb, w_in, b_in, lo):
    return jax.nn.sigmoid(_dot(hb, w_in[:, lo:lo + D_MODEL]) + b_in[:, lo:lo + D_MODEL])


def _mix_out(h, gate_attn, y_attn, gate_conv, y_conv, w_out):
    mixed = (gate_attn * y_attn + gate_conv * y_conv).astype(BF16)
    return DEEPNORM_ALPHA * h + _dot(mixed, w_out[...])


def _ffn(*args):
    result = []
    for _ in _ffn_stages(*args, result.append):
        pass
    return result[0]


def _ffn_stages(m_rows, r1, ln1g, ln1b, w_up, fdw, fdwb, w_down, ln2g, ln2b, ub, upcarry, up_out,
                emit):
    h1 = _layer_norm(r1, ln1g[...], ln1b[...])
    h1b = h1.astype(BF16)
    half = FFN_SLABS // 2
    ahead = FFN_BUFS - 1

    def chunk_cols(j, s):
        base = j * FFN_CHUNK + (s % half) * LANES + (D_FF if s >= half else 0)
        return slice(base, base + LANES)

    def up_proj(j):
        return (_dot(h1b, w_up[:, j * FFN_CHUNK:(j + 1) * FFN_CHUNK]),
                _dot(h1b, w_up[:, D_FF + j * FFN_CHUNK:D_FF + (j + 1) * FFN_CHUNK]))

    acc = jnp.zeros((m_rows, D_MODEL), F32)
    ups = {j: up_proj(j) for j in range(ahead)}
    yield
    for j in range(N_FFN_CHUNKS):
        u, g = ups.pop(j)
        buf = j % FFN_BUFS
        if up_out is not None:
            up_out[:, j * FFN_CHUNK:(j + 1) * FFN_CHUNK] = u
            up_out[:, D_FF + j * FFN_CHUNK:D_FF + (j + 1) * FFN_CHUNK] = g
        for s in range(FFN_SLABS):
            src = u if s < half else g
            ls = slice((s % half) * LANES, (s % half + 1) * LANES)
            ub[buf, s, _tok(0, FFN_HIST), :] = upcarry[j, s]
            ub[buf, s, _tok(FFN_HIST, m_rows), :] = src[:, ls]
            upcarry[j, s] = ub[buf, s, _tok(m_rows, FFN_HIST), :]
        if j + ahead < N_FFN_CHUNKS:
            ups[j + ahead] = up_proj(j + ahead)
        conv = [_conv3_slab(ub, buf, s, m_rows, fdw[:, chunk_cols(j, s)], fdwb[:, chunk_cols(j, s)])
                for s in range(FFN_SLABS)]
        f = jnp.concatenate([_gelu(conv[half + s]) * conv[s] for s in range(half)], axis=1)
        acc = acc + _dot(f.astype(BF16), w_down[j * FFN_CHUNK:(j + 1) * FFN_CHUNK, :])
        yield
    emit(_layer_norm(DEEPNORM_ALPHA * h1 + acc, ln2g[...], ln2b[...]))


def _main_kernel(apply_in_ln, n_tiles, tiles_per_seq,
                 h_ref, ilg, ilb, w_in, b_in, sink_t, bband_t, bmeta_t, kmeta, vmeta_t, cmeta, upmeta,
                 w_ap, cdw, cdwb, clng, clnb, w_cp, w_out, ln1g, ln1b,
                 w_up, fdw, fdwb, w_down, ln2g, ln2b,
                 out_ref,
                 q_buf, k_seq, vt_seq, cb, a_buf, h1s, ub, upcarry):
    n = pl.program_id(0)
    t_front = jnp.minimum(n, n_tiles - 1) % tiles_per_seq
    t_back = jnp.maximum(n - 1, 0) % tiles_per_seq
    wslot = n % 2
    rslot = 1 - wslot

    @pl.when(n == 0)
    def _():
        h1s[rslot] = jnp.zeros((TM, D_MODEL), F32)

    @pl.when(t_front == 0)
    def _():
        k_seq[0:BLOCK, :] = jnp.zeros((BLOCK, KV_WIDTH), BF16)
        vt_seq[0] = jnp.zeros((KV_WIDTH, BLOCK), BF16)
        for s in range(CONV_SLABS):
            cb[s, _tok(0, CONV_HIST - N_META), :] = jnp.zeros((CONV_HIST - N_META, LANES), F32)
            cb[s, _tok(CONV_HIST - N_META, N_META), :] = cmeta[:, s * LANES:(s + 1) * LANES]

    @pl.when(t_back == 0)
    def _():
        half = FFN_SLABS // 2
        for j in range(N_FFN_CHUNKS):
            for s in range(FFN_SLABS):
                base = j * FFN_CHUNK + (s % half) * LANES + (D_FF if s >= half else 0)
                upcarry[j, s] = upmeta[N_META - FFN_HIST:N_META, base:base + LANES]

    first = (t_front == 0).astype(jnp.int32)
    blk0 = t_front * (TM // BLOCK)
    k_row0 = pl.multiple_of(blk0 * BLOCK, BLOCK)

    def scores(j, kvh):
        r0 = j * BLOCK
        k_band = k_seq[pl.ds(k_row0 + r0, 2 * BLOCK), :]
        return _attn_scores_t(q_buf[r0:r0 + BLOCK, :], kvh, k_band, kmeta[...])

    def finish(j, kvh, sc):
        r0 = j * BLOCK
        bidx = first if j == 0 else 0
        vt_band = jnp.concatenate([vt_seq[blk0 + j], vt_seq[blk0 + j + 1]], axis=1)
        o = _attn_finish_t(sc, kvh, vt_band, vmeta_t[...], bband_t[bidx, kvh], bmeta_t[bidx, kvh],
                           sink_t[kvh])
        a_buf[r0:r0 + BLOCK, kvh * GROUP_WIDTH:(kvh + 1) * GROUP_WIDTH] = o.astype(BF16)

    def conv_proj(r0):
        cvals = [_conv_branch_rows(cb, r, CONV_ROWS, cdw, cdwb, clng, clnb)
                 for r in range(r0, r0 + BLOCK, CONV_ROWS)]
        return _dot(jnp.concatenate(cvals, axis=0), w_cp[...])

    def mixer_half():
        h = h_ref[0]
        if apply_in_ln:
            h = _layer_norm(h, ilg[...], ilb[...])
        hb, qkv, c = _in_proj(h, w_in, b_in)
        q_buf[...] = (qkv[:, :ATTN_WIDTH] * (HEAD_DIM ** -0.5)).astype(BF16)
        k_seq[pl.ds(k_row0 + BLOCK, TM), :] = qkv[:, ATTN_WIDTH:ATTN_WIDTH + KV_WIDTH].astype(BF16)
        v = qkv[:, ATTN_WIDTH + KV_WIDTH:QKV_END]
        for jb in range(TM // BLOCK):
            vt_seq[blk0 + 1 + jb] = v[jb * BLOCK:(jb + 1) * BLOCK, :].T.astype(BF16)
        for s in range(CONV_SLABS):
            cb[s, _tok(CONV_HIST, TM), :] = c[:, s * LANES:(s + 1) * LANES]
        yield
        units = [(j, kvh) for j in range(TM // BLOCK) for kvh in range(N_KV_HEADS)]
        assert len(units) == 4
        sc0 = scores(*units[0])
        sc1 = scores(*units[1])
        yield
        gate_attn = _gate(hb, w_in, b_in, CIN_END)
        yield
        finish(*units[0], sc0)
        sc2 = scores(*units[2])
        yield
        finish(*units[1], sc1)
        sc3 = scores(*units[3])
        y_conv_top = conv_proj(0)
        yield
        gate_conv = _gate(hb, w_in, b_in, CIN_END + D_MODEL)
        yield
        finish(*units[2], sc2)
        yield
        finish(*units[3], sc3)
        y_conv_bot = conv_proj(BLOCK)
        yield
        y_attn = _dot(a_buf[...], w_ap[...])
        yield
        y_conv = jnp.concatenate([y_conv_top, y_conv_bot], axis=0)
        h1s[wslot] = _mix_out(h, gate_attn, y_attn, gate_conv, y_conv, w_out)

    def store_out(y):
        out_ref[0] = y

    ffn_half = _ffn_stages(TM, h1s[rslot], ln1g, ln1b, w_up, fdw, fdwb, w_down, ln2g, ln2b,
                           ub, upcarry, None, store_out)
    order = "MFMFMFMFMFMFMFMFFFMFFFM"
    halves = {"M": mixer_half(), "F": ffn_half}
    for which in order:
        next(halves[which], None)
    for gen in halves.values():
        assert next(gen, "done") == "done"

    for s in range(CONV_SLABS):
        cb[s, _tok(0, CONV_HIST), :] = cb[s, _tok(TM, CONV_HIST), :]


def _meta_kernel(apply_in_ln,
                 h_ref, ilg, ilb, w_in, b_in, sinks, bmetaq,
                 w_ap, cdw, cdwb, clng, clnb, w_cp, w_out, ln1g, ln1b,
                 w_up, fdw, fdwb, w_down, ln2g, ln2b,
                 hout_ref, kvpad_ref, cmeta_ref, upmeta_ref,
                 cb, ub, upcarry):
    h = h_ref[...]
    if apply_in_ln:
        h = _layer_norm(h, ilg[...], ilb[...])
    hb, qkv, c = _in_proj(h, w_in, b_in)
    q = (qkv[:, :ATTN_WIDTH] * (HEAD_DIM ** -0.5)).astype(BF16)
    kvpad_ref[...] = jnp.zeros((BLOCK, 2 * KV_WIDTH), BF16)
    kvpad_ref[0:N_META, :] = qkv[:, ATTN_WIDTH:QKV_END].astype(BF16)
    cmeta_ref[...] = c
    for s in range(CONV_SLABS):
        cb[s, _tok(0, CONV_HIST), :] = jnp.zeros((CONV_HIST, LANES), F32)
        cb[s, _tok(CONV_HIST, N_META), :] = c[:, s * LANES:(s + 1) * LANES]
    upcarry[...] = jnp.zeros(upcarry.shape, F32)

    meta_kv = kvpad_ref[...]
    o = jnp.concatenate(
        [_attn_finish(_attn_scores(q, kvh, None, meta_kv), N_META, kvh, None, meta_kv, None,
                      bmetaq[kvh * GQA_GROUP:(kvh + 1) * GQA_GROUP].reshape(GQA_GROUP * N_META, BLOCK),
                      sinks)
         for kvh in range(N_KV_HEADS)], axis=1)
    conv_b = _conv_branch_rows(cb, 0, N_META, cdw, cdwb, clng, clnb)
    y_attn = _dot(o.astype(BF16), w_ap[...])
    y_conv = _dot(conv_b, w_cp[...])
    r1 = _mix_out(h, _gate(hb, w_in, b_in, CIN_END), y_attn,
                  _gate(hb, w_in, b_in, CIN_END + D_MODEL), y_conv, w_out)
    hout_ref[...] = _ffn(N_META, r1, ln1g, ln1b, w_up, fdw, fdwb, w_down, ln2g, ln2b, ub, upcarry,
                         upmeta_ref)


def _t5_bucket(d):
    n = jnp.maximum(d, 0)
    max_exact = REL_BUCKETS // 2
    nf = jnp.maximum(n, 1).astype(F32)
    large = max_exact + (jnp.log(nf / max_exact) / math.log(REL_MAX_DIST / max_exact)
                         * (REL_BUCKETS - max_exact)).astype(jnp.int32)
    large = jnp.minimum(large, REL_BUCKETS - 1)
    return jnp.where(n < max_exact, n, large)


def _bias_tables(rel_bias):
    rb = rel_bias.astype(F32)

    def bias_of(d, valid):
        onehot = jax.nn.one_hot(_t5_bucket(d), REL_BUCKETS, dtype=F32)
        b = jnp.einsum("qkb,bh->hqk", onehot, rb, precision=lax.Precision.HIGHEST)
        return jnp.where(valid[None], b, MASKED)

    i = jnp.arange(BLOCK)[:, None]
    j = jnp.arange(2 * BLOCK)[None, :]
    d = i + BLOCK - j
    valid = (d >= 0) & (d < WINDOW)
    band = jnp.stack([bias_of(d, valid), bias_of(d, valid & (j >= BLOCK))])
    far = N_META + BLOCK - (N_META - 1)
    assert 16 + int(math.log(far / 16) / math.log(REL_MAX_DIST / 16) * 16) >= REL_BUCKETS - 1
    mcol = jnp.arange(BLOCK)[None, :]
    mvalid = jnp.broadcast_to(mcol < N_META, (BLOCK, BLOCK))
    meta = jnp.stack([bias_of(N_META + BLOCK + i - mcol, mvalid),
                      bias_of(N_META + i - mcol, mvalid)])
    qi = jnp.arange(N_META)[:, None]
    metaq = bias_of(qi - mcol, (mcol <= qi) & (mcol < N_META))

    def keys_on_rows(tbl, n_keys):
        t = tbl[..., :n_keys].reshape(2, N_KV_HEADS, GQA_GROUP, BLOCK, n_keys)
        return t.transpose(0, 1, 4, 2, 3).reshape(2, N_KV_HEADS, n_keys, GQA_GROUP * BLOCK)

    return keys_on_rows(band, 2 * BLOCK), keys_on_rows(meta, N_META), metaq


def _vmem_spec():
    return pl.BlockSpec(memory_space=pltpu.VMEM)


def _conv_scratch(m_rows):
    return [
        pltpu.VMEM((CONV_SLABS, 2 * (CONV_HIST + m_rows), LANES), F32),
        pltpu.VMEM((FFN_BUFS, FFN_SLABS, 2 * (FFN_HIST + m_rows), LANES), F32),
        pltpu.VMEM((N_FFN_CHUNKS, FFN_SLABS, FFN_HIST, LANES), F32),
    ]


def _meta_layer(apply_in_ln, hm, ilg, ilb, lw, bmetaq):
    n_up = 2 * D_FF
    out_shape = (jax.ShapeDtypeStruct((N_META, D_MODEL), F32),
                 jax.ShapeDtypeStruct((BLOCK, 2 * KV_WIDTH), BF16),
                 jax.ShapeDtypeStruct((N_META, CONV_WIDTH), F32),
                 jax.ShapeDtypeStruct((N_META, n_up), F32))
    ins = (hm, ilg, ilb, lw["w_in"], lw["b_in"], lw["sinks"], bmetaq,
           lw["w_ap"], lw["cdw"], lw["cdwb"], lw["clng"], lw["clnb"], lw["w_cp"], lw["w_out"],
           lw["ln1g"], lw["ln1b"], lw["w_up"], lw["fdw"], lw["fdwb"], lw["w_down"],
           lw["ln2g"], lw["ln2b"])
    in_specs = [_vmem_spec() for _ in ins]
    in_specs[5] = pl.BlockSpec(memory_space=pltpu.SMEM)
    return pl.pallas_call(
        functools.partial(_meta_kernel, apply_in_ln),
        out_shape=out_shape,
        in_specs=in_specs,
        out_specs=tuple(_vmem_spec() for _ in out_shape),
        scratch_shapes=_conv_scratch(N_META),
        compiler_params=pltpu.CompilerParams(vmem_limit_bytes=VMEM_LIMIT_BYTES),
        name="meta_layer",
    )(*ins)


def _main_layer(apply_in_ln, h, ilg, ilb, lw, bband_t, bmeta_t, kvmeta, cmeta, upmeta):
    batch, seq, _ = h.shape
    assert seq % TM == 0 and TM % BLOCK == 0 and TM % CONV_ROWS == 0
    sink_t = jnp.repeat(lw["sinks"].reshape(N_KV_HEADS, 1, GQA_GROUP), BLOCK, axis=2)
    kmeta = kvmeta[:N_META, :KV_WIDTH]
    vmeta_t = kvmeta[:, KV_WIDTH:].T
    ins = (h, ilg, ilb, lw["w_in"], lw["b_in"], sink_t, bband_t, bmeta_t, kmeta, vmeta_t, cmeta, upmeta,
           lw["w_ap"], lw["cdw"], lw["cdwb"], lw["clng"], lw["clnb"], lw["w_cp"], lw["w_out"],
           lw["ln1g"], lw["ln1b"], lw["w_up"], lw["fdw"], lw["fdwb"], lw["w_down"],
           lw["ln2g"], lw["ln2b"])
    tiles_per_seq = seq // TM
    n_tiles = batch * tiles_per_seq

    def tile_of(n):
        return (n // tiles_per_seq, n % tiles_per_seq, 0)

    in_tile = pl.BlockSpec((1, TM, D_MODEL), lambda n: tile_of(jnp.minimum(n, n_tiles - 1)))
    out_tile = pl.BlockSpec((1, TM, D_MODEL), lambda n: tile_of(jnp.maximum(n - 1, 0)))
    in_specs = [_vmem_spec() for _ in ins]
    in_specs[0] = in_tile
    return pl.pallas_call(
        functools.partial(_main_kernel, apply_in_ln, n_tiles, tiles_per_seq),
        out_shape=jax.ShapeDtypeStruct((batch, seq, D_MODEL), F32),
        grid=(n_tiles + 1,),
        in_specs=in_specs,
        out_specs=out_tile,
        scratch_shapes=[
            pltpu.VMEM((TM, ATTN_WIDTH), BF16),
            pltpu.VMEM((BLOCK + seq, KV_WIDTH), BF16),
            pltpu.VMEM((1 + seq // BLOCK, KV_WIDTH, BLOCK), BF16),
        ] + _conv_scratch(TM)[:1] + [
            pltpu.VMEM((TM, ATTN_WIDTH), BF16),
            pltpu.VMEM((2, TM, D_MODEL), F32),
        ] + _conv_scratch(TM)[1:],
        compiler_params=pltpu.CompilerParams(
            dimension_semantics=("arbitrary",),
            vmem_limit_bytes=VMEM_LIMIT_BYTES),
        name="main_layer",
    )(*ins)


def _layer_weights(i, w_in, b_in, attn_sinks, w_attn_proj, conv_dw, conv_dw_b, conv_ln_g,
                   conv_ln_b, w_conv_proj, w_out, ln1_g, ln1_b, ffn_w_up, ffn_dw, ffn_dw_b,
                   ffn_w_down, ln2_g, ln2_b):
    row = lambda v: v[i][None, :].astype(F32)
    return dict(
        w_in=w_in[i].astype(BF16), b_in=row(b_in), sinks=attn_sinks[i].astype(F32),
        w_ap=w_attn_proj[i].astype(BF16), cdw=conv_dw[i].astype(F32), cdwb=row(conv_dw_b),
        clng=row(conv_ln_g), clnb=row(conv_ln_b), w_cp=w_conv_proj[i].astype(BF16),
        w_out=w_out[i].astype(BF16), ln1g=row(ln1_g), ln1b=row(ln1_b),
        w_up=ffn_w_up[i].astype(BF16), fdw=ffn_dw[i].astype(F32), fdwb=row(ffn_dw_b),
        w_down=ffn_w_down[i].astype(BF16), ln2g=row(ln2_g), ln2b=row(ln2_b))


def kernel(x, meta_tokens, in_ln_g, in_ln_b, rel_bias, w_in, b_in, attn_sinks, w_attn_proj, conv_dw, conv_dw_b, conv_ln_g, conv_ln_b, w_conv_proj, w_out, ln1_g, ln1_b, ffn_w_up, ffn_dw, ffn_dw_b, ffn_w_down, ln2_g, ln2_b):
    bband, bmeta, bmetaq = _bias_tables(rel_bias)
    ilg = in_ln_g[None, :].astype(F32)
    ilb = in_ln_b[None, :].astype(F32)
    h = x.astype(F32)
    hm = meta_tokens.astype(F32)
    for i in range(DEPTH):
        lw = _layer_weights(i, w_in, b_in, attn_sinks, w_attn_proj, conv_dw, conv_dw_b,
                            conv_ln_g, conv_ln_b, w_conv_proj, w_out, ln1_g, ln1_b, ffn_w_up,
                            ffn_dw, ffn_dw_b, ffn_w_down, ln2_g, ln2_b)
        hm_next, kvmeta, cmeta, upmeta = _meta_layer(i == 0, hm, ilg, ilb, lw, bmetaq)
        h = _main_layer(i == 0, h, ilg, ilb, lw, bband, bmeta, kvmeta, cmeta, upmeta)
        hm = hm_next
    return h
```

```python
import functools
import math

import jax
import jax.numpy as jnp
from jax import lax
from jax.experimental import pallas as pl
from jax.experimental.pallas import tpu as pltpu

D_MODEL = 1024
DEPTH = 2
N_META = 16
HEAD_DIM = 64
N_Q_HEADS = 8
N_KV_HEADS = 2
GQA_GROUP = N_Q_HEADS // N_KV_HEADS
ATTN_WIDTH = N_Q_HEADS * HEAD_DIM
KV_WIDTH = N_KV_HEADS * HEAD_DIM
GROUP_WIDTH = GQA_GROUP * HEAD_DIM
WINDOW = 128
BLOCK = 128
CONV_WIDTH = D_MODEL // 2
CONV_TAPS = 31
REL_BUCKETS = 32
REL_MAX_DIST = 128
D_FF = 2816
FFN_TAPS = 3
LN_EPS = 1e-5
DEEPNORM_ALPHA = (2.0 * DEPTH) ** 0.25
IN_COLS = ATTN_WIDTH + 2 * KV_WIDTH + 2 * CONV_WIDTH + 2 * D_MODEL

QKV_END = ATTN_WIDTH + 2 * KV_WIDTH
CIN_END = QKV_END + 2 * CONV_WIDTH

LANES = 128
TM = 256
FFN_CHUNK = 256
N_FFN_CHUNKS = D_FF // FFN_CHUNK
FFN_SLABS = 2 * FFN_CHUNK // LANES
FFN_BUFS = 3
CONV_SLABS = CONV_WIDTH // LANES
CONV_HIST = 32
FFN_HIST = 8
CONV_ROWS = 64
MASKED = -1e30
VMEM_LIMIT_BYTES = 60000 * 1024

BF16 = jnp.bfloat16
F32 = jnp.float32


def _dot(a, b):
    return jnp.dot(a, b, preferred_element_type=F32)


def _dot_nt(a, b):
    return lax.dot_general(a, b, (((1,), (1,)), ((), ())), preferred_element_type=F32)


def _layer_norm(x, g, b):
    mu = jnp.mean(x, axis=-1, keepdims=True)
    xc = x - mu
    var = jnp.mean(xc * xc, axis=-1, keepdims=True)
    return xc * lax.rsqrt(var + LN_EPS) * g + b


def _gelu2(x):
    one = jnp.asarray(1.0, x.dtype)
    return x * (one + lax.erf(x * jnp.asarray(math.sqrt(0.5), x.dtype)))


def _tok(start, n):
    return pl.ds(2 * start, n, stride=2)


def _qkv_proj(hb, w_in, b_in):
    return _dot(hb, w_in[:, 0:QKV_END]) + b_in[:, 0:QKV_END]


def _glu_proj(hb, w_in, b_in):
    cin = _dot(hb, w_in[:, QKV_END:CIN_END]) + b_in[:, QKV_END:CIN_END]
    return cin[:, :CONV_WIDTH] * jax.nn.sigmoid(cin[:, CONV_WIDTH:])


def _attn_scores(q_blk, kvh, band_kv, meta_kv):
    heads = range(kvh * GQA_GROUP, (kvh + 1) * GQA_GROUP)
    ks = slice(kvh * HEAD_DIM, (kvh + 1) * HEAD_DIM)
    q = jnp.concatenate([q_blk[:, a * HEAD_DIM:(a + 1) * HEAD_DIM] for a in heads], axis=0)
    s_m = _dot_nt(q, meta_kv[:, ks])
    s_b = None if band_kv is None else _dot_nt(q, band_kv[:, ks])
    return s_m, s_b


def _attn_finish(scores, rows, kvh, band_kv, meta_kv, band_bias, meta_bias, sinks):
    s_m, s_b = scores
    heads = range(kvh * GQA_GROUP, (kvh + 1) * GQA_GROUP)
    vs = slice(KV_WIDTH + kvh * HEAD_DIM, KV_WIDTH + (kvh + 1) * HEAD_DIM)
    sink = jnp.concatenate([jnp.full((rows, 1), sinks[a], F32) for a in heads], axis=0)
    s_m = s_m + meta_bias
    m = jnp.maximum(jnp.max(s_m, axis=-1, keepdims=True), sink)
    if band_kv is not None:
        s_b = s_b + band_bias
        m = jnp.maximum(m, jnp.max(s_b, axis=-1, keepdims=True))
    p_m = jnp.exp(s_m - m)
    den = jnp.sum(p_m, axis=-1, keepdims=True) + jnp.exp(sink - m)
    o = _dot(p_m.astype(BF16), meta_kv[:, vs])
    if band_kv is not None:
        p_b = jnp.exp(s_b - m)
        den = den + jnp.sum(p_b, axis=-1, keepdims=True)
        o = o + _dot(p_b.astype(BF16), band_kv[:, vs])
    o = o / den
    return jnp.concatenate([o[g * rows:(g + 1) * rows] for g in range(GQA_GROUP)], axis=1)


def _attn_scores_t(q_blk, kvh, k_band, k_meta):
    heads = range(kvh * GQA_GROUP, (kvh + 1) * GQA_GROUP)
    ks = slice(kvh * HEAD_DIM, (kvh + 1) * HEAD_DIM)
    q = jnp.concatenate([q_blk[:, a * HEAD_DIM:(a + 1) * HEAD_DIM] for a in heads], axis=0)
    k_all = jnp.concatenate([k_band[:, ks], k_meta[:, ks]], axis=0)
    return _dot_nt(k_all, q)


def _attn_finish_t(s_t, kvh, vt_band, vt_meta, band_bias, meta_bias, sink):
    vrows = slice(kvh * HEAD_DIM, (kvh + 1) * HEAD_DIM)
    cols = s_t.shape[1]
    s_b = s_t[:2 * BLOCK] + band_bias
    s_m = s_t[2 * BLOCK:] + meta_bias
    m = jnp.maximum(jnp.maximum(jnp.max(s_b, axis=0, keepdims=True),
                                jnp.max(s_m, axis=0, keepdims=True)), sink)
    p_b = jnp.exp(s_b - m)
    p_m = jnp.exp(s_m - m)
    den = (jnp.sum(p_b, axis=0, keepdims=True) + jnp.sum(p_m, axis=0, keepdims=True)
           + jnp.exp(sink - m))
    p_m_pad = jnp.concatenate([p_m.astype(BF16), jnp.zeros((BLOCK - N_META, cols), BF16)], axis=0)
    o_t = (_dot(vt_band[vrows], p_b.astype(BF16)) + _dot(vt_meta[vrows], p_m_pad)) / den
    stacked = jnp.concatenate([o_t[:, g * BLOCK:(g + 1) * BLOCK] for g in range(GQA_GROUP)], axis=0)
    return stacked.T


def _conv31_rows(cb, r0, rows, cdw, cdwb):
    first = CONV_HIST - (CONV_TAPS - 1)
    slabs = []
    for s in range(CONV_SLABS):
        ls = slice(s * LANES, (s + 1) * LANES)
        acc = jnp.broadcast_to(cdwb[:, ls], (rows, LANES))
        for k in range(CONV_TAPS):
            acc = acc + cdw[k:k + 1, ls] * cb[s, _tok(r0 + first + k, rows), :]
        slabs.append(acc)
    return jnp.concatenate(slabs, axis=1)


def _conv_branch_rows(cb, r0, rows, cdw, cdwb, clng, clnb):
    cc = _conv31_rows(cb, r0, rows, cdw, cdwb)
    return jax.nn.silu(_layer_norm(cc, clng[...], clnb[...])).astype(BF16)


def _conv3_slab(ub, buf, slab, m_rows, w, b):
    first = FFN_HIST - (FFN_TAPS - 1)
    acc = b
    for k in range(FFN_TAPS):
        acc = acc + w[k:k + 1, :] * ub[buf, slab, _tok(first + k, m_rows), :]
    return acc


def _gate(hb, w_in, b_in, lo):
    return jax.nn.sigmoid(_dot(hb, w_in[:, lo:lo + D_MODEL]) + b_in[:, lo:lo + D_MODEL])


def _mix_out(h, gate_attn, y_attn, gate_conv, y_conv, w_out):
    mixed = (gate_attn * y_attn + gate_conv * y_conv).astype(BF16)
    return DEEPNORM_ALPHA * h + _dot(mixed, w_out[...])


def _ffn(*args):
    result = []
    for _ in _ffn_stages(*args, result.append):
        pass
    return result[0]


def _ffn_stages(m_rows, r1, ln1g, ln1b, w_up, fdw, fdwb, w_down, ln2g, ln2b, ub, upcarry, up_out,
                emit):
    h1 = _layer_norm(r1, ln1g[...], ln1b[...])
    h1b = h1.astype(BF16)
    half = FFN_SLABS // 2
    ahead = FFN_BUFS - 1

    def chunk_cols(j, s):
        base = j * FFN_CHUNK + (s % half) * LANES + (D_FF if s >= half else 0)
        return slice(base, base + LANES)

    def up_proj(j):
        return (_dot(h1b, w_up[:, j * FFN_CHUNK:(j + 1) * FFN_CHUNK]),
                _dot(h1b, w_up[:, D_FF + j * FFN_CHUNK:D_FF + (j + 1) * FFN_CHUNK]))

    acc = jnp.zeros((m_rows, D_MODEL), F32)
    ups = {j: up_proj(j) for j in range(ahead)}
    yield
    for j in range(N_FFN_CHUNKS):
        u, g = ups.pop(j)
        buf = j % FFN_BUFS
        if up_out is not None:
            up_out[:, j * FFN_CHUNK:(j + 1) * FFN_CHUNK] = u
            up_out[:, D_FF + j * FFN_CHUNK:D_FF + (j + 1) * FFN_CHUNK] = g
        for s in range(FFN_SLABS):
            src = u if s < half else g
            ls = slice((s % half) * LANES, (s % half + 1) * LANES)
            ub[buf, s, _tok(0, FFN_HIST), :] = upcarry[j, s]
            ub[buf, s, _tok(FFN_HIST, m_rows), :] = src[:, ls]
            upcarry[j, s] = ub[buf, s, _tok(m_rows, FFN_HIST), :]
        if j + ahead < N_FFN_CHUNKS:
            ups[j + ahead] = up_proj(j + ahead)
        conv = [_conv3_slab(ub, buf, s, m_rows, fdw[:, chunk_cols(j, s)], fdwb[:, chunk_cols(j, s)])
                for s in range(FFN_SLABS)]
        f = jnp.concatenate([_gelu2(conv[half + s].astype(BF16)) * conv[s].astype(BF16)
                             for s in range(half)], axis=1)
        acc = acc + _dot(f, w_down[j * FFN_CHUNK:(j + 1) * FFN_CHUNK, :])
        yield
    emit(_layer_norm(DEEPNORM_ALPHA * h1 + acc, ln2g[...], ln2b[...]))


def _main_kernel(apply_in_ln, n_tiles, tiles_per_seq,
                 h_ref, ilg, ilb, w_in, b_in, sink_t, bband_t, bmeta_t, kmeta, vmeta_t, cmeta, upmeta,
                 w_ap, cdw, cdwb, clng, clnb, w_cp, w_out, ln1g, ln1b,
                 w_up, fdw, fdwb, w_down, ln2g, ln2b,
                 out_ref,
                 q_buf, k_seq, vt_seq, cb, a_buf, h1s, ub, upcarry):
    n = pl.program_id(0)
    t_front = jnp.minimum(n, n_tiles - 1) % tiles_per_seq
    t_back = jnp.maximum(n - 1, 0) % tiles_per_seq
    wslot = n % 2
    rslot = 1 - wslot

    @pl.when(n == 0)
    def _():
        h1s[rslot] = jnp.zeros((TM, D_MODEL), F32)

    @pl.when(t_front == 0)
    def _():
        k_seq[0:BLOCK, :] = jnp.zeros((BLOCK, KV_WIDTH), BF16)
        vt_seq[0] = jnp.zeros((KV_WIDTH, BLOCK), BF16)
        for s in range(CONV_SLABS):
            cb[s, _tok(0, CONV_HIST - N_META), :] = jnp.zeros((CONV_HIST - N_META, LANES), F32)
            cb[s, _tok(CONV_HIST - N_META, N_META), :] = cmeta[:, s * LANES:(s + 1) * LANES]

    @pl.when(t_back == 0)
    def _():
        half = FFN_SLABS // 2
        for j in range(N_FFN_CHUNKS):
            for s in range(FFN_SLABS):
                base = j * FFN_CHUNK + (s % half) * LANES + (D_FF if s >= half else 0)
                upcarry[j, s] = upmeta[N_META - FFN_HIST:N_META, base:base + LANES]

    first = (t_front == 0).astype(jnp.int32)
    blk0 = t_front * (TM // BLOCK)
    k_row0 = pl.multiple_of(blk0 * BLOCK, BLOCK)

    def scores(j, kvh):
        r0 = j * BLOCK
        k_band = k_seq[pl.ds(k_row0 + r0, 2 * BLOCK), :]
        return _attn_scores_t(q_buf[r0:r0 + BLOCK, :], kvh, k_band, kmeta[...])

    def finish(j, kvh, sc):
        r0 = j * BLOCK
        bidx = first if j == 0 else 0
        vt_band = jnp.concatenate([vt_seq[blk0 + j], vt_seq[blk0 + j + 1]], axis=1)
        o = _attn_finish_t(sc, kvh, vt_band, vmeta_t[...], bband_t[bidx, kvh], bmeta_t[bidx, kvh],
                           sink_t[kvh])
        a_buf[r0:r0 + BLOCK, kvh * GROUP_WIDTH:(kvh + 1) * GROUP_WIDTH] = o.astype(BF16)

    def conv_proj(r0):
        cvals = [_conv_branch_rows(cb, r, CONV_ROWS, cdw, cdwb, clng, clnb)
                 for r in range(r0, r0 + BLOCK, CONV_ROWS)]
        return _dot(jnp.concatenate(cvals, axis=0), w_cp[...])

    def mixer_half():
        h = h_ref[0]
        if apply_in_ln:
            h = _layer_norm(h, ilg[...], ilb[...])
        hb = h.astype(BF16)
        qkv = _qkv_proj(hb, w_in, b_in)
        q_buf[...] = (qkv[:, :ATTN_WIDTH] * (HEAD_DIM ** -0.5)).astype(BF16)
        k_seq[pl.ds(k_row0 + BLOCK, TM), :] = qkv[:, ATTN_WIDTH:ATTN_WIDTH + KV_WIDTH].astype(BF16)
        v = qkv[:, ATTN_WIDTH + KV_WIDTH:QKV_END]
        for jb in range(TM // BLOCK):
            vt_seq[blk0 + 1 + jb] = v[jb * BLOCK:(jb + 1) * BLOCK, :].T.astype(BF16)

        c = _glu_proj(hb, w_in, b_in)
        for s in range(CONV_SLABS):
            cb[s, _tok(CONV_HIST, TM), :] = c[:, s * LANES:(s + 1) * LANES]
        yield
        units = [(j, kvh) for j in range(TM // BLOCK) for kvh in range(N_KV_HEADS)]
        assert len(units) == 4
        sc0 = scores(*units[0])
        sc1 = scores(*units[1])
        sc2 = scores(*units[2])
        sc3 = scores(*units[3])
        yield
        gate_attn = _gate(hb, w_in, b_in, CIN_END)
        yield
        gate_conv = _gate(hb, w_in, b_in, CIN_END + D_MODEL)
        yield
        finish(*units[0], sc0)
        yield
        finish(*units[1], sc1)
        y_conv_top = conv_proj(0)
        yield
        finish(*units[2], sc2)
        yield
        finish(*units[3], sc3)
        y_conv_bot = conv_proj(BLOCK)
        yield
        y_attn = _dot(a_buf[...], w_ap[...])
        yield
        y_conv = jnp.concatenate([y_conv_top, y_conv_bot], axis=0)
        h1s[wslot] = _mix_out(h, gate_attn, y_attn, gate_conv, y_conv, w_out)

    def store_out(y):
        out_ref[0] = y

    ffn_half = _ffn_stages(TM, h1s[rslot], ln1g, ln1b, w_up, fdw, fdwb, w_down, ln2g, ln2b,
                           ub, upcarry, None, store_out)
    order = "MFMMMFMFMFMFMFFFFMFFFFM"
    halves = {"M": mixer_half(), "F": ffn_half}
    for which in order:
        next(halves[which], None)
    for gen in halves.values():
        assert next(gen, "done") == "done"

    for s in range(CONV_SLABS):
        cb[s, _tok(0, CONV_HIST), :] = cb[s, _tok(TM, CONV_HIST), :]


def _meta_kernel(apply_in_ln,
                 h_ref, ilg, ilb, w_in, b_in, sinks, bmetaq,
                 w_ap, cdw, cdwb, clng, clnb, w_cp, w_out, ln1g, ln1b,
                 w_up, fdw, fdwb, w_down, ln2g, ln2b,
                 hout_ref, kvpad_ref, cmeta_ref, upmeta_ref,
                 cb, ub, upcarry):
    h = h_ref[...]
    if apply_in_ln:
        h = _layer_norm(h, ilg[...], ilb[...])
    hb = h.astype(BF16)
    qkv = _qkv_proj(hb, w_in, b_in)
    c = _glu_proj(hb, w_in, b_in)
    q = (qkv[:, :ATTN_WIDTH] * (HEAD_DIM ** -0.5)).astype(BF16)
    kvpad_ref[...] = jnp.zeros((BLOCK, 2 * KV_WIDTH), BF16)
    kvpad_ref[0:N_META, :] = qkv[:, ATTN_WIDTH:QKV_END].astype(BF16)
    cmeta_ref[...] = c
    for s in range(CONV_SLABS):
        cb[s, _tok(0, CONV_HIST), :] = jnp.zeros((CONV_HIST, LANES), F32)
        cb[s, _tok(CONV_HIST, N_META), :] = c[:, s * LANES:(s + 1) * LANES]
    upcarry[...] = jnp.zeros(upcarry.shape, F32)

    meta_kv = kvpad_ref[...]
    o = jnp.concatenate(
        [_attn_finish(_attn_scores(q, kvh, None, meta_kv), N_META, kvh, None, meta_kv, None,
                      bmetaq[kvh * GQA_GROUP:(kvh + 1) * GQA_GROUP].reshape(GQA_GROUP * N_META, BLOCK),
                      sinks)
         for kvh in range(N_KV_HEADS)], axis=1)
    conv_b = _conv_branch_rows(cb, 0, N_META, cdw, cdwb, clng, clnb)
    y_attn = _dot(o.astype(BF16), w_ap[...])
    y_conv = _dot(conv_b, w_cp[...])
    r1 = _mix_out(h, _gate(hb, w_in, b_in, CIN_END), y_attn,
                  _gate(hb, w_in, b_in, CIN_END + D_MODEL), y_conv, w_out)
    hout_ref[...] = _ffn(N_META, r1, ln1g, ln1b, w_up, fdw, fdwb, w_down, ln2g, ln2b, ub, upcarry,
                         upmeta_ref)


def _t5_bucket(d):
    n = jnp.maximum(d, 0)
    max_exact = REL_BUCKETS // 2
    nf = jnp.maximum(n, 1).astype(F32)
    large = max_exact + (jnp.log(nf / max_exact) / math.log(REL_MAX_DIST / max_exact)
                         * (REL_BUCKETS - max_exact)).astype(jnp.int32)
    large = jnp.minimum(large, REL_BUCKETS - 1)
    return jnp.where(n < max_exact, n, large)


def _bias_tables(rel_bias):
    rb = rel_bias.astype(F32)

    def bias_of(d, valid):
        onehot = jax.nn.one_hot(_t5_bucket(d), REL_BUCKETS, dtype=F32)
        b = jnp.einsum("qkb,bh->hqk", onehot, rb, precision=lax.Precision.HIGHEST)
        return jnp.where(valid[None], b, MASKED)

    i = jnp.arange(BLOCK)[:, None]
    j = jnp.arange(2 * BLOCK)[None, :]
    d = i + BLOCK - j
    valid = (d >= 0) & (d < WINDOW)
    band = jnp.stack([bias_of(d, valid), bias_of(d, valid & (j >= BLOCK))])
    far = N_META + BLOCK - (N_META - 1)
    assert 16 + int(math.log(far / 16) / math.log(REL_MAX_DIST / 16) * 16) >= REL_BUCKETS - 1
    mcol = jnp.arange(BLOCK)[None, :]
    mvalid = jnp.broadcast_to(mcol < N_META, (BLOCK, BLOCK))
    meta = jnp.stack([bias_of(N_META + BLOCK + i - mcol, mvalid),
                      bias_of(N_META + i - mcol, mvalid)])
    qi = jnp.arange(N_META)[:, None]
    metaq = bias_of(qi - mcol, (mcol <= qi) & (mcol < N_META))

    def keys_on_rows(tbl, n_keys):
        t = tbl[..., :n_keys].reshape(2, N_KV_HEADS, GQA_GROUP, BLOCK, n_keys)
        return t.transpose(0, 1, 4, 2, 3).reshape(2, N_KV_HEADS, n_keys, GQA_GROUP * BLOCK)

    return keys_on_rows(band, 2 * BLOCK), keys_on_rows(meta, N_META), metaq


def _vmem_spec():
    return pl.BlockSpec(memory_space=pltpu.VMEM)


def _conv_scratch(m_rows):
    return [
        pltpu.VMEM((CONV_SLABS, 2 * (CONV_HIST + m_rows), LANES), F32),
        pltpu.VMEM((FFN_BUFS, FFN_SLABS, 2 * (FFN_HIST + m_rows), LANES), F32),
        pltpu.VMEM((N_FFN_CHUNKS, FFN_SLABS, FFN_HIST, LANES), F32),
    ]


def _meta_layer(apply_in_ln, hm, ilg, ilb, lw, bmetaq):
    n_up = 2 * D_FF
    out_shape = (jax.ShapeDtypeStruct((N_META, D_MODEL), F32),
                 jax.ShapeDtypeStruct((BLOCK, 2 * KV_WIDTH), BF16),
                 jax.ShapeDtypeStruct((N_META, CONV_WIDTH), F32),
                 jax.ShapeDtypeStruct((N_META, n_up), F32))
    ins = (hm, ilg, ilb, lw["w_in"], lw["b_in"], lw["sinks"], bmetaq,
           lw["w_ap"], lw["cdw"], lw["cdwb"], lw["clng"], lw["clnb"], lw["w_cp"], lw["w_out"],
           lw["ln1g"], lw["ln1b"], lw["w_up"], lw["fdw"], lw["fdwb"], lw["w_down"],
           lw["ln2g"], lw["ln2b"])
    in_specs = [_vmem_spec() for _ in ins]
    in_specs[5] = pl.BlockSpec(memory_space=pltpu.SMEM)
    return pl.pallas_call(
        functools.partial(_meta_kernel, apply_in_ln),
        out_shape=out_shape,
        in_specs=in_specs,
        out_specs=tuple(_vmem_spec() for _ in out_shape),
        scratch_shapes=_conv_scratch(N_META),
        compiler_params=pltpu.CompilerParams(vmem_limit_bytes=VMEM_LIMIT_BYTES),
        name="meta_layer",
    )(*ins)


def _main_layer(apply_in_ln, h, ilg, ilb, lw, bband_t, bmeta_t, kvmeta, cmeta, upmeta):
    batch, seq, _ = h.shape
    assert seq % TM == 0 and TM % BLOCK == 0 and TM % CONV_ROWS == 0
    sink_t = jnp.repeat(lw["sinks"].reshape(N_KV_HEADS, 1, GQA_GROUP), BLOCK, axis=2)
    kmeta = kvmeta[:N_META, :KV_WIDTH]
    vmeta_t = kvmeta[:, KV_WIDTH:].T
    ins = (h, ilg, ilb, lw["w_in"], lw["b_in"], sink_t, bband_t, bmeta_t, kmeta, vmeta_t, cmeta, upmeta,
           lw["w_ap"], lw["cdw"], lw["cdwb"], lw["clng"], lw["clnb"], lw["w_cp"], lw["w_out"],
           lw["ln1g"], lw["ln1b"], lw["w_up"], lw["fdw"], lw["fdwb"], lw["w_down"],
           lw["ln2g"], lw["ln2b"])
    tiles_per_seq = seq // TM
    n_tiles = batch * tiles_per_seq

    def tile_of(n):
        return (n // tiles_per_seq, n % tiles_per_seq, 0)

    in_tile = pl.BlockSpec((1, TM, D_MODEL), lambda n: tile_of(jnp.minimum(n, n_tiles - 1)))
    out_tile = pl.BlockSpec((1, TM, D_MODEL), lambda n: tile_of(jnp.maximum(n - 1, 0)))
    in_specs = [_vmem_spec() for _ in ins]
    in_specs[0] = in_tile
    return pl.pallas_call(
        functools.partial(_main_kernel, apply_in_ln, n_tiles, tiles_per_seq),
        out_shape=jax.ShapeDtypeStruct((batch, seq, D_MODEL), F32),
        grid=(n_tiles + 1,),
        in_specs=in_specs,
        out_specs=out_tile,
        scratch_shapes=[
            pltpu.VMEM((TM, ATTN_WIDTH), BF16),
            pltpu.VMEM((BLOCK + seq, KV_WIDTH), BF16),
            pltpu.VMEM((1 + seq // BLOCK, KV_WIDTH, BLOCK), BF16),
        ] + _conv_scratch(TM)[:1] + [
            pltpu.VMEM((TM, ATTN_WIDTH), BF16),
            pltpu.VMEM((2, TM, D_MODEL), F32),
        ] + _conv_scratch(TM)[1:],
        compiler_params=pltpu.CompilerParams(
            dimension_semantics=("arbitrary",),
            vmem_limit_bytes=VMEM_LIMIT_BYTES),
        name="main_layer",
    )(*ins)


def _layer_weights(i, w_in, b_in, attn_sinks, w_attn_proj, conv_dw, conv_dw_b, conv_ln_g,
                   conv_ln_b, w_conv_proj, w_out, ln1_g, ln1_b, ffn_w_up, ffn_dw, ffn_dw_b,
                   ffn_w_down, ln2_g, ln2_b):
    row = lambda v: v[i][None, :].astype(F32)
    return dict(
        w_in=w_in[i].astype(BF16), b_in=row(b_in), sinks=attn_sinks[i].astype(F32),
        w_ap=w_attn_proj[i].astype(BF16), cdw=conv_dw[i].astype(F32), cdwb=row(conv_dw_b),
        clng=row(conv_ln_g), clnb=row(conv_ln_b), w_cp=w_conv_proj[i].astype(BF16),
        w_out=w_out[i].astype(BF16), ln1g=row(ln1_g), ln1b=row(ln1_b),
        w_up=ffn_w_up[i].astype(BF16), fdw=ffn_dw[i].astype(F32), fdwb=row(ffn_dw_b),
        w_down=(0.5 * ffn_w_down[i]).astype(BF16), ln2g=row(ln2_g), ln2b=row(ln2_b))


def kernel(x, meta_tokens, in_ln_g, in_ln_b, rel_bias, w_in, b_in, attn_sinks, w_attn_proj, conv_dw, conv_dw_b, conv_ln_g, conv_ln_b, w_conv_proj, w_out, ln1_g, ln1_b, ffn_w_up, ffn_dw, ffn_dw_b, ffn_w_down, ln2_g, ln2_b):
    bband, bmeta, bmetaq = _bias_tables(rel_bias)
    ilg = in_ln_g[None, :].astype(F32)
    ilb = in_ln_b[None, :].astype(F32)
    h = x.astype(F32)
    hm = meta_tokens.astype(F32)
    for i in range(DEPTH):
        lw = _layer_weights(i, w_in, b_in, attn_sinks, w_attn_proj, conv_dw, conv_dw_b,
                            conv_ln_g, conv_ln_b, w_conv_proj, w_out, ln1_g, ln1_b, ffn_w_up,
                            ffn_dw, ffn_dw_b, ffn_w_down, ln2_g, ln2_b)
        hm_next, kvmeta, cmeta, upmeta = _meta_layer(i == 0, hm, ilg, ilb, lw, bmetaq)
        h = _main_layer(i == 0, h, ilg, ilb, lw, bband, bmeta, kvmeta, cmeta, upmeta)
        hm = hm_next
    return h
```

```python
import functools
import math

import jax
import jax.numpy as jnp
from jax import lax
from jax.experimental import pallas as pl
from jax.experimental.pallas import tpu as pltpu

D_MODEL = 1024
DEPTH = 2
N_META = 16
HEAD_DIM = 64
N_Q_HEADS = 8
N_KV_HEADS = 2
GQA_GROUP = N_Q_HEADS // N_KV_HEADS
ATTN_WIDTH = N_Q_HEADS * HEAD_DIM
KV_WIDTH = N_KV_HEADS * HEAD_DIM
GROUP_WIDTH = GQA_GROUP * HEAD_DIM
WINDOW = 128
BLOCK = 128
CONV_WIDTH = D_MODEL // 2
CONV_TAPS = 31
REL_BUCKETS = 32
REL_MAX_DIST = 128
D_FF = 2816
FFN_TAPS = 3
LN_EPS = 1e-5
DEEPNORM_ALPHA = (2.0 * DEPTH) ** 0.25
IN_COLS = ATTN_WIDTH + 2 * KV_WIDTH + 2 * CONV_WIDTH + 2 * D_MODEL

QKV_END = ATTN_WIDTH + 2 * KV_WIDTH
CIN_END = QKV_END + 2 * CONV_WIDTH

LANES = 128
TM = 256
FFN_CHUNK = 256
N_FFN_CHUNKS = D_FF // FFN_CHUNK
FFN_SLABS = 2 * FFN_CHUNK // LANES
FFN_BUFS = 3
CONV_SLABS = CONV_WIDTH // LANES
CONV_HIST = 32
FFN_HIST = 8
CONV_ROWS = 64
MASKED = -1e30
VMEM_LIMIT_BYTES = 60000 * 1024

BF16 = jnp.bfloat16
F32 = jnp.float32


def _dot(a, b):
    return jnp.dot(a, b, preferred_element_type=F32)


def _dot_nt(a, b):
    return lax.dot_general(a, b, (((1,), (1,)), ((), ())), preferred_element_type=F32)


def _layer_norm(x, g, b):
    mu = jnp.mean(x, axis=-1, keepdims=True)
    xc = x - mu
    var = jnp.mean(xc * xc, axis=-1, keepdims=True)
    return xc * lax.rsqrt(var + LN_EPS) * g + b


def _gelu2(x):
    one = jnp.asarray(1.0, x.dtype)
    return x * (one + lax.erf(x * jnp.asarray(math.sqrt(0.5), x.dtype)))


def _tok(start, n):
    return pl.ds(2 * start, n, stride=2)


def _qkv_proj(hb, w_in, b_in):
    return _dot(hb, w_in[:, 0:QKV_END]) + b_in[:, 0:QKV_END]


def _glu_proj(hb, w_in, b_in):
    cin = _dot(hb, w_in[:, QKV_END:CIN_END]) + b_in[:, QKV_END:CIN_END]
    return cin[:, :CONV_WIDTH] * jax.nn.sigmoid(cin[:, CONV_WIDTH:])


def _attn_scores(q_blk, kvh, band_kv, meta_kv):
    heads = range(kvh * GQA_GROUP, (kvh + 1) * GQA_GROUP)
    ks = slice(kvh * HEAD_DIM, (kvh + 1) * HEAD_DIM)
    q = jnp.concatenate([q_blk[:, a * HEAD_DIM:(a + 1) * HEAD_DIM] for a in heads], axis=0)
    s_m = _dot_nt(q, meta_kv[:, ks])
    s_b = None if band_kv is None else _dot_nt(q, band_kv[:, ks])
    return s_m, s_b


def _attn_finish(scores, rows, kvh, band_kv, meta_kv, band_bias, meta_bias, sinks):
    s_m, s_b = scores
    heads = range(kvh * GQA_GROUP, (kvh + 1) * GQA_GROUP)
    vs = slice(KV_WIDTH + kvh * HEAD_DIM, KV_WIDTH + (kvh + 1) * HEAD_DIM)
    sink = jnp.concatenate([jnp.full((rows, 1), sinks[a], F32) for a in heads], axis=0)
    s_m = s_m + meta_bias
    m = jnp.maximum(jnp.max(s_m, axis=-1, keepdims=True), sink)
    if band_kv is not None:
        s_b = s_b + band_bias
        m = jnp.maximum(m, jnp.max(s_b, axis=-1, keepdims=True))
    p_m = jnp.exp(s_m - m)
    den = jnp.sum(p_m, axis=-1, keepdims=True) + jnp.exp(sink - m)
    o = _dot(p_m.astype(BF16), meta_kv[:, vs])
    if band_kv is not None:
        p_b = jnp.exp(s_b - m)
        den = den + jnp.sum(p_b, axis=-1, keepdims=True)
        o = o + _dot(p_b.astype(BF16), band_kv[:, vs])
    o = o / den
    return jnp.concatenate([o[g * rows:(g + 1) * rows] for g in range(GQA_GROUP)], axis=1)


def _attn_scores_t(q_blk, kvh, k_band, k_meta):
    heads = range(kvh * GQA_GROUP, (kvh + 1) * GQA_GROUP)
    ks = slice(kvh * HEAD_DIM, (kvh + 1) * HEAD_DIM)
    q = jnp.concatenate([q_blk[:, a * HEAD_DIM:(a + 1) * HEAD_DIM] for a in heads], axis=0)
    k_all = jnp.concatenate([k_band[:, ks], k_meta[:, ks]], axis=0)
    return _dot_nt(k_all, q)


def _attn_finish_t(s_t, kvh, vt_band, vt_meta, band_bias, meta_bias, sink):
    vrows = slice(kvh * HEAD_DIM, (kvh + 1) * HEAD_DIM)
    cols = s_t.shape[1]
    s_b = s_t[:2 * BLOCK] + band_bias
    s_m = s_t[2 * BLOCK:] + meta_bias
    m = jnp.maximum(jnp.maximum(jnp.max(s_b, axis=0, keepdims=True),
                                jnp.max(s_m, axis=0, keepdims=True)), sink)
    p_b = jnp.exp(s_b - m)
    p_m = jnp.exp(s_m - m)
    den = (jnp.sum(p_b, axis=0, keepdims=True) + jnp.sum(p_m, axis=0, keepdims=True)
           + jnp.exp(sink - m))
    p_m_pad = jnp.concatenate([p_m.astype(BF16), jnp.zeros((BLOCK - N_META, cols), BF16)], axis=0)
    o_t = (_dot(vt_band[vrows], p_b.astype(BF16)) + _dot(vt_meta[vrows], p_m_pad)) / den
    stacked = jnp.concatenate([o_t[:, g * BLOCK:(g + 1) * BLOCK] for g in range(GQA_GROUP)], axis=0)
    return stacked.T


def _conv31_rows(cb, r0, rows, cdw, cdwb):
    first = CONV_HIST - (CONV_TAPS - 1)
    slabs = []
    for s in range(CONV_SLABS):
        ls = slice(s * LANES, (s + 1) * LANES)
        acc = jnp.broadcast_to(cdwb[:, ls], (rows, LANES))
        for k in range(CONV_TAPS):
            acc = acc + cdw[k:k + 1, ls] * cb[s, _tok(r0 + first + k, rows), :]
        slabs.append(acc)
    return jnp.concatenate(slabs, axis=1)


def _conv_branch_rows(cb, r0, rows, cdw, cdwb, clng, clnb):
    cc = _conv31_rows(cb, r0, rows, cdw, cdwb)
    return jax.nn.silu(_layer_norm(cc, clng[...], clnb[...])).astype(BF16)


def _conv3_slab(ub, buf, slab, m_rows, w, b):
    first = FFN_HIST - (FFN_TAPS - 1)
    acc = b
    for k in range(FFN_TAPS):
        acc = acc + w[k:k + 1, :] * ub[buf, slab, _tok(first + k, m_rows), :]
    return acc


def _gate(hb, w_in, b_in, lo):
    return jax.nn.sigmoid(_dot(hb, w_in[:, lo:lo + D_MODEL]) + b_in[:, lo:lo + D_MODEL])


def _mix_out(h, gate_attn, y_attn, gate_conv, y_conv, w_out):
    mixed = (gate_attn * y_attn + gate_conv * y_conv).astype(BF16)
    return DEEPNORM_ALPHA * h + _dot(mixed, w_out[...])


def _ffn(*args):
    result = []
    for _ in _ffn_stages(*args, result.append):
        pass
    return result[0]


def _ffn_stages(m_rows, r1, ln1g, ln1b, w_up, fdw, fdwb, w_down, ln2g, ln2b, ub, upcarry, up_out,
                emit):
    h1 = _layer_norm(r1, ln1g[...], ln1b[...])
    h1b = h1.astype(BF16)
    half = FFN_SLABS // 2
    ahead = FFN_BUFS - 1

    def chunk_cols(j, s):
        base = j * FFN_CHUNK + (s % half) * LANES + (D_FF if s >= half else 0)
        return slice(base, base + LANES)

    def up_proj(j):
        return (_dot(h1b, w_up[:, j * FFN_CHUNK:(j + 1) * FFN_CHUNK]),
                _dot(h1b, w_up[:, D_FF + j * FFN_CHUNK:D_FF + (j + 1) * FFN_CHUNK]))

    acc = jnp.zeros((m_rows, D_MODEL), F32)
    ups = {j: up_proj(j) for j in range(ahead)}
    yield
    for j in range(N_FFN_CHUNKS):
        u, g = ups.pop(j)
        buf = j % FFN_BUFS
        if up_out is not None:
            up_out[:, j * FFN_CHUNK:(j + 1) * FFN_CHUNK] = u
            up_out[:, D_FF + j * FFN_CHUNK:D_FF + (j + 1) * FFN_CHUNK] = g
        for s in range(FFN_SLABS):
            src = u if s < half else g
            ls = slice((s % half) * LANES, (s % half + 1) * LANES)
            ub[buf, s, _tok(0, FFN_HIST), :] = upcarry[j, s]
            ub[buf, s, _tok(FFN_HIST, m_rows), :] = src[:, ls]
            upcarry[j, s] = ub[buf, s, _tok(m_rows, FFN_HIST), :]
        if j + ahead < N_FFN_CHUNKS:
            ups[j + ahead] = up_proj(j + ahead)
        conv = [_conv3_slab(ub, buf, s, m_rows, fdw[:, chunk_cols(j, s)], fdwb[:, chunk_cols(j, s)])
                for s in range(FFN_SLABS)]
        f = jnp.concatenate([_gelu2(conv[half + s].astype(BF16)) * conv[s].astype(BF16)
                             for s in range(half)], axis=1)
        acc = acc + _dot(f, w_down[j * FFN_CHUNK:(j + 1) * FFN_CHUNK, :])
        yield
    emit(_layer_norm(DEEPNORM_ALPHA * h1 + acc, ln2g[...], ln2b[...]))


def _main_kernel(apply_in_ln, n_tiles, tiles_per_seq,
                 h_ref, ilg, ilb, w_in, b_in, sink_t, bband_t, bmeta_t, kmeta, vmeta_t, cmeta, upmeta,
                 w_ap, cdw, cdwb, clng, clnb, w_cp, w_out, ln1g, ln1b,
                 w_up, fdw, fdwb, w_down, ln2g, ln2b,
                 out_ref,
                 q_buf, k_seq, vt_seq, cb, a_buf, h1s, ub, upcarry):
    n = pl.program_id(0)
    t_front = jnp.minimum(n, n_tiles - 1) % tiles_per_seq
    t_back = jnp.maximum(n - 1, 0) % tiles_per_seq
    wslot = n % 2
    rslot = 1 - wslot

    @pl.when(n == 0)
    def _():
        h1s[rslot] = jnp.zeros((TM, D_MODEL), F32)

    @pl.when(t_front == 0)
    def _():
        k_seq[0:BLOCK, :] = jnp.zeros((BLOCK, KV_WIDTH), BF16)
        vt_seq[0] = jnp.zeros((KV_WIDTH, BLOCK), BF16)
        for s in range(CONV_SLABS):
            cb[s, _tok(0, CONV_HIST - N_META), :] = jnp.zeros((CONV_HIST - N_META, LANES), F32)
            cb[s, _tok(CONV_HIST - N_META, N_META), :] = cmeta[:, s * LANES:(s + 1) * LANES]

    @pl.when(t_back == 0)
    def _():
        half = FFN_SLABS // 2
        for j in range(N_FFN_CHUNKS):
            for s in range(FFN_SLABS):
                base = j * FFN_CHUNK + (s % half) * LANES + (D_FF if s >= half else 0)
                upcarry[j, s] = upmeta[N_META - FFN_HIST:N_META, base:base + LANES]

    first = (t_front == 0).astype(jnp.int32)
    blk0 = t_front * (TM // BLOCK)
    k_row0 = pl.multiple_of(blk0 * BLOCK, BLOCK)

    def scores(j, kvh):
        r0 = j * BLOCK
        k_band = k_seq[pl.ds(k_row0 + r0, 2 * BLOCK), :]
        return _attn_scores_t(q_buf[r0:r0 + BLOCK, :], kvh, k_band, kmeta[...])

    def finish(j, kvh, sc):
        r0 = j * BLOCK
        bidx = first if j == 0 else 0
        vt_band = jnp.concatenate([vt_seq[blk0 + j], vt_seq[blk0 + j + 1]], axis=1)
        o = _attn_finish_t(sc, kvh, vt_band, vmeta_t[...], bband_t[bidx, kvh], bmeta_t[bidx, kvh],
                           sink_t[kvh])
        a_buf[r0:r0 + BLOCK, kvh * GROUP_WIDTH:(kvh + 1) * GROUP_WIDTH] = o.astype(BF16)

    def conv_proj(r0):
        cvals = [_conv_branch_rows(cb, r, CONV_ROWS, cdw, cdwb, clng, clnb)
                 for r in range(r0, r0 + BLOCK, CONV_ROWS)]
        return _dot(jnp.concatenate(cvals, axis=0), w_cp[...])

    def mixer_half():
        h = h_ref[0].astype(F32)
        if apply_in_ln:
            h = _layer_norm(h, ilg[...], ilb[...])
        hb = h.astype(BF16)
        qkv = _qkv_proj(hb, w_in, b_in)
        q_buf[...] = (qkv[:, :ATTN_WIDTH] * (HEAD_DIM ** -0.5)).astype(BF16)
        k_seq[pl.ds(k_row0 + BLOCK, TM), :] = qkv[:, ATTN_WIDTH:ATTN_WIDTH + KV_WIDTH].astype(BF16)
        v = qkv[:, ATTN_WIDTH + KV_WIDTH:QKV_END]
        for jb in range(TM // BLOCK):
            vt_seq[blk0 + 1 + jb] = v[jb * BLOCK:(jb + 1) * BLOCK, :].T.astype(BF16)

        c = _glu_proj(hb, w_in, b_in)
        for s in range(CONV_SLABS):
            cb[s, _tok(CONV_HIST, TM), :] = c[:, s * LANES:(s + 1) * LANES]
        yield
        units = [(j, kvh) for j in range(TM // BLOCK) for kvh in range(N_KV_HEADS)]
        assert len(units) == 4
        sc0 = scores(*units[0])
        sc1 = scores(*units[1])
        sc2 = scores(*units[2])
        sc3 = scores(*units[3])
        yield
        gate_attn = _gate(hb, w_in, b_in, CIN_END)
        yield
        gate_conv = _gate(hb, w_in, b_in, CIN_END + D_MODEL)
        yield
        finish(*units[0], sc0)
        yield
        finish(*units[1], sc1)
        y_conv_top = conv_proj(0)
        yield
        finish(*units[2], sc2)
        yield
        finish(*units[3], sc3)
        y_conv_bot = conv_proj(BLOCK)
        yield
        y_attn = _dot(a_buf[...], w_ap[...])
        yield
        y_conv = jnp.concatenate([y_conv_top, y_conv_bot], axis=0)
        h1s[wslot] = _mix_out(h, gate_attn, y_attn, gate_conv, y_conv, w_out)

    def store_out(y):
        out_ref[0] = y.astype(out_ref.dtype)

    ffn_half = _ffn_stages(TM, h1s[rslot], ln1g, ln1b, w_up, fdw, fdwb, w_down, ln2g, ln2b,
                           ub, upcarry, None, store_out)
    order = "MFMMMFMFMFMFMFFFFMFFFFM"
    halves = {"M": mixer_half(), "F": ffn_half}
    for which in order:
        next(halves[which], None)
    for gen in halves.values():
        assert next(gen, "done") == "done"

    for s in range(CONV_SLABS):
        cb[s, _tok(0, CONV_HIST), :] = cb[s, _tok(TM, CONV_HIST), :]


def _meta_kernel(apply_in_ln,
                 h_ref, ilg, ilb, w_in, b_in, sinks, bmetaq,
                 w_ap, cdw, cdwb, clng, clnb, w_cp, w_out, ln1g, ln1b,
                 w_up, fdw, fdwb, w_down, ln2g, ln2b,
                 hout_ref, kvpad_ref, cmeta_ref, upmeta_ref,
                 cb, ub, upcarry):
    h = h_ref[...]
    if apply_in_ln:
        h = _layer_norm(h, ilg[...], ilb[...])
    hb = h.astype(BF16)
    qkv = _qkv_proj(hb, w_in, b_in)
    c = _glu_proj(hb, w_in, b_in)
    q = (qkv[:, :ATTN_WIDTH] * (HEAD_DIM ** -0.5)).astype(BF16)
    kvpad_ref[...] = jnp.zeros((BLOCK, 2 * KV_WIDTH), BF16)
    kvpad_ref[0:N_META, :] = qkv[:, ATTN_WIDTH:QKV_END].astype(BF16)
    cmeta_ref[...] = c
    for s in range(CONV_SLABS):
        cb[s, _tok(0, CONV_HIST), :] = jnp.zeros((CONV_HIST, LANES), F32)
        cb[s, _tok(CONV_HIST, N_META), :] = c[:, s * LANES:(s + 1) * LANES]
    upcarry[...] = jnp.zeros(upcarry.shape, F32)

    meta_kv = kvpad_ref[...]
    o = jnp.concatenate(
        [_attn_finish(_attn_scores(q, kvh, None, meta_kv), N_META, kvh, None, meta_kv, None,
                      bmetaq[kvh * GQA_GROUP:(kvh + 1) * GQA_GROUP].reshape(GQA_GROUP * N_META, BLOCK),
                      sinks)
         for kvh in range(N_KV_HEADS)], axis=1)
    conv_b = _conv_branch_rows(cb, 0, N_META, cdw, cdwb, clng, clnb)
    y_attn = _dot(o.astype(BF16), w_ap[...])
    y_conv = _dot(conv_b, w_cp[...])
    r1 = _mix_out(h, _gate(hb, w_in, b_in, CIN_END), y_attn,
                  _gate(hb, w_in, b_in, CIN_END + D_MODEL), y_conv, w_out)
    hout_ref[...] = _ffn(N_META, r1, ln1g, ln1b, w_up, fdw, fdwb, w_down, ln2g, ln2b, ub, upcarry,
                         upmeta_ref)


def _t5_bucket(d):
    n = jnp.maximum(d, 0)
    max_exact = REL_BUCKETS // 2
    nf = jnp.maximum(n, 1).astype(F32)
    large = max_exact + (jnp.log(nf / max_exact) / math.log(REL_MAX_DIST / max_exact)
                         * (REL_BUCKETS - max_exact)).astype(jnp.int32)
    large = jnp.minimum(large, REL_BUCKETS - 1)
    return jnp.where(n < max_exact, n, large)


def _bias_tables(rel_bias):
    rb = rel_bias.astype(F32)

    def bias_of(d, valid):
        onehot = jax.nn.one_hot(_t5_bucket(d), REL_BUCKETS, dtype=F32)
        b = jnp.einsum("qkb,bh->hqk", onehot, rb, precision=lax.Precision.HIGHEST)
        return jnp.where(valid[None], b, MASKED)

    i = jnp.arange(BLOCK)[:, None]
    j = jnp.arange(2 * BLOCK)[None, :]
    d = i + BLOCK - j
    valid = (d >= 0) & (d < WINDOW)
    band = jnp.stack([bias_of(d, valid), bias_of(d, valid & (j >= BLOCK))])
    far = N_META + BLOCK - (N_META - 1)
    assert 16 + int(math.log(far / 16) / math.log(REL_MAX_DIST / 16) * 16) >= REL_BUCKETS - 1
    mcol = jnp.arange(BLOCK)[None, :]
    mvalid = jnp.broadcast_to(mcol < N_META, (BLOCK, BLOCK))
    meta = jnp.stack([bias_of(N_META + BLOCK + i - mcol, mvalid),
                      bias_of(N_META + i - mcol, mvalid)])
    qi = jnp.arange(N_META)[:, None]
    metaq = bias_of(qi - mcol, (mcol <= qi) & (mcol < N_META))

    def keys_on_rows(tbl, n_keys):
        t = tbl[..., :n_keys].reshape(2, N_KV_HEADS, GQA_GROUP, BLOCK, n_keys)
        return t.transpose(0, 1, 4, 2, 3).reshape(2, N_KV_HEADS, n_keys, GQA_GROUP * BLOCK)

    return keys_on_rows(band, 2 * BLOCK), keys_on_rows(meta, N_META), metaq


def _vmem_spec():
    return pl.BlockSpec(memory_space=pltpu.VMEM)


def _conv_scratch(m_rows):
    return [
        pltpu.VMEM((CONV_SLABS, 2 * (CONV_HIST + m_rows), LANES), F32),
        pltpu.VMEM((FFN_BUFS, FFN_SLABS, 2 * (FFN_HIST + m_rows), LANES), F32),
        pltpu.VMEM((N_FFN_CHUNKS, FFN_SLABS, FFN_HIST, LANES), F32),
    ]


def _meta_layer(apply_in_ln, hm, ilg, ilb, lw, bmetaq):
    n_up = 2 * D_FF
    out_shape = (jax.ShapeDtypeStruct((N_META, D_MODEL), F32),
                 jax.ShapeDtypeStruct((BLOCK, 2 * KV_WIDTH), BF16),
                 jax.ShapeDtypeStruct((N_META, CONV_WIDTH), F32),
                 jax.ShapeDtypeStruct((N_META, n_up), F32))
    ins = (hm, ilg, ilb, lw["w_in"], lw["b_in"], lw["sinks"], bmetaq,
           lw["w_ap"], lw["cdw"], lw["cdwb"], lw["clng"], lw["clnb"], lw["w_cp"], lw["w_out"],
           lw["ln1g"], lw["ln1b"], lw["w_up"], lw["fdw"], lw["fdwb"], lw["w_down"],
           lw["ln2g"], lw["ln2b"])
    in_specs = [_vmem_spec() for _ in ins]
    in_specs[5] = pl.BlockSpec(memory_space=pltpu.SMEM)
    return pl.pallas_call(
        functools.partial(_meta_kernel, apply_in_ln),
        out_shape=out_shape,
        in_specs=in_specs,
        out_specs=tuple(_vmem_spec() for _ in out_shape),
        scratch_shapes=_conv_scratch(N_META),
        compiler_params=pltpu.CompilerParams(vmem_limit_bytes=VMEM_LIMIT_BYTES),
        name="meta_layer",
    )(*ins)


def _main_layer(apply_in_ln, out_dtype, h, ilg, ilb, lw, bband_t, bmeta_t, kvmeta, cmeta, upmeta):
    batch, seq, _ = h.shape
    assert seq % TM == 0 and TM % BLOCK == 0 and TM % CONV_ROWS == 0
    sink_t = jnp.repeat(lw["sinks"].reshape(N_KV_HEADS, 1, GQA_GROUP), BLOCK, axis=2)
    kmeta = kvmeta[:N_META, :KV_WIDTH]
    vmeta_t = kvmeta[:, KV_WIDTH:].T
    ins = (h, ilg, ilb, lw["w_in"], lw["b_in"], sink_t, bband_t, bmeta_t, kmeta, vmeta_t, cmeta, upmeta,
           lw["w_ap"], lw["cdw"], lw["cdwb"], lw["clng"], lw["clnb"], lw["w_cp"], lw["w_out"],
           lw["ln1g"], lw["ln1b"], lw["w_up"], lw["fdw"], lw["fdwb"], lw["w_down"],
           lw["ln2g"], lw["ln2b"])
    tiles_per_seq = seq // TM
    n_tiles = batch * tiles_per_seq

    def tile_of(n):
        return (n // tiles_per_seq, n % tiles_per_seq, 0)

    in_tile = pl.BlockSpec((1, TM, D_MODEL), lambda n: tile_of(jnp.minimum(n, n_tiles - 1)))
    out_tile = pl.BlockSpec((1, TM, D_MODEL), lambda n: tile_of(jnp.maximum(n - 1, 0)))
    in_specs = [_vmem_spec() for _ in ins]
    in_specs[0] = in_tile
    return pl.pallas_call(
        functools.partial(_main_kernel, apply_in_ln, n_tiles, tiles_per_seq),
        out_shape=jax.ShapeDtypeStruct((batch, seq, D_MODEL), out_dtype),
        grid=(n_tiles + 1,),
        in_specs=in_specs,
        out_specs=out_tile,
        scratch_shapes=[
            pltpu.VMEM((TM, ATTN_WIDTH), BF16),
            pltpu.VMEM((BLOCK + seq, KV_WIDTH), BF16),
            pltpu.VMEM((1 + seq // BLOCK, KV_WIDTH, BLOCK), BF16),
        ] + _conv_scratch(TM)[:1] + [
            pltpu.VMEM((TM, ATTN_WIDTH), BF16),
            pltpu.VMEM((2, TM, D_MODEL), F32),
        ] + _conv_scratch(TM)[1:],
        compiler_params=pltpu.CompilerParams(
            dimension_semantics=("arbitrary",),
            vmem_limit_bytes=VMEM_LIMIT_BYTES),
        name="main_layer",
    )(*ins)


def _layer_weights(i, w_in, b_in, attn_sinks, w_attn_proj, conv_dw, conv_dw_b, conv_ln_g,
                   conv_ln_b, w_conv_proj, w_out, ln1_g, ln1_b, ffn_w_up, ffn_dw, ffn_dw_b,
                   ffn_w_down, ln2_g, ln2_b):
    row = lambda v: v[i][None, :].astype(F32)
    return dict(
        w_in=w_in[i].astype(BF16), b_in=row(b_in), sinks=attn_sinks[i].astype(F32),
        w_ap=w_attn_proj[i].astype(BF16), cdw=conv_dw[i].astype(F32), cdwb=row(conv_dw_b),
        clng=row(conv_ln_g), clnb=row(conv_ln_b), w_cp=w_conv_proj[i].astype(BF16),
        w_out=w_out[i].astype(BF16), ln1g=row(ln1_g), ln1b=row(ln1_b),
        w_up=ffn_w_up[i].astype(BF16), fdw=ffn_dw[i].astype(F32), fdwb=row(ffn_dw_b),
        w_down=(0.5 * ffn_w_down[i]).astype(BF16), ln2g=row(ln2_g), ln2b=row(ln2_b))


def kernel(x, meta_tokens, in_ln_g, in_ln_b, rel_bias, w_in, b_in, attn_sinks, w_attn_proj, conv_dw, conv_dw_b, conv_ln_g, conv_ln_b, w_conv_proj, w_out, ln1_g, ln1_b, ffn_w_up, ffn_dw, ffn_dw_b, ffn_w_down, ln2_g, ln2_b):
    bband, bmeta, bmetaq = _bias_tables(rel_bias)
    ilg = in_ln_g[None, :].astype(F32)
    ilb = in_ln_b[None, :].astype(F32)
    h = x.astype(F32)
    hm = meta_tokens.astype(F32)
    for i in range(DEPTH):
        lw = _layer_weights(i, w_in, b_in, attn_sinks, w_attn_proj, conv_dw, conv_dw_b,
                            conv_ln_g, conv_ln_b, w_conv_proj, w_out, ln1_g, ln1_b, ffn_w_up,
                            ffn_dw, ffn_dw_b, ffn_w_down, ln2_g, ln2_b)
        hm_next, kvmeta, cmeta, upmeta = _meta_layer(i == 0, hm, ilg, ilb, lw, bmetaq)
        out_dtype = F32 if i == DEPTH - 1 else BF16
        h = _main_layer(i == 0, out_dtype, h, ilg, ilb, lw, bband, bmeta, kvmeta, cmeta, upmeta)
        hm = hm_next
    return h
```

```python
import functools
import math

import jax
import jax.numpy as jnp
from jax import lax
from jax.experimental import pallas as pl
from jax.experimental.pallas import tpu as pltpu

D_MODEL = 1024
DEPTH = 2
N_META = 16
HEAD_DIM = 64
N_Q_HEADS = 8
N_KV_HEADS = 2
GQA_GROUP = N_Q_HEADS // N_KV_HEADS
ATTN_WIDTH = N_Q_HEADS * HEAD_DIM
KV_WIDTH = N_KV_HEADS * HEAD_DIM
GROUP_WIDTH = GQA_GROUP * HEAD_DIM
WINDOW = 128
BLOCK = 128
CONV_WIDTH = D_MODEL // 2
CONV_TAPS = 31
REL_BUCKETS = 32
REL_MAX_DIST = 128
D_FF = 2816
FFN_TAPS = 3
LN_EPS = 1e-5
DEEPNORM_ALPHA = (2.0 * DEPTH) ** 0.25
IN_COLS = ATTN_WIDTH + 2 * KV_WIDTH + 2 * CONV_WIDTH + 2 * D_MODEL

QKV_END = ATTN_WIDTH + 2 * KV_WIDTH
CIN_END = QKV_END + 2 * CONV_WIDTH

LANES = 128
TM = 256
FFN_CHUNK = 256
N_FFN_CHUNKS = D_FF // FFN_CHUNK
FFN_SLABS = 2 * FFN_CHUNK // LANES
FFN_BUFS = 3
CONV_SLABS = CONV_WIDTH // LANES
CONV_HIST = 32
FFN_HIST = 8
CONV_ROWS = 64
MASKED = -1e30
VMEM_LIMIT_BYTES = 60000 * 1024

BF16 = jnp.bfloat16
F32 = jnp.float32


def _dot(a, b):
    return jnp.dot(a, b, preferred_element_type=F32)


def _dot_nt(a, b):
    return lax.dot_general(a, b, (((1,), (1,)), ((), ())), preferred_element_type=F32)


def _layer_norm(x, g, b):
    mu = jnp.mean(x, axis=-1, keepdims=True)
    xc = x - mu
    var = jnp.mean(xc * xc, axis=-1, keepdims=True)
    return xc * lax.rsqrt(var + LN_EPS) * g + b


def _gelu2(x):
    one = jnp.asarray(1.0, x.dtype)
    return x * (one + lax.erf(x * jnp.asarray(math.sqrt(0.5), x.dtype)))


def _tok(start, n):
    return pl.ds(2 * start, n, stride=2)


def _qkv_proj(hb, w_in, b_in):
    return _dot(hb, w_in[:, 0:QKV_END]) + b_in[:, 0:QKV_END]


def _glu_proj(hb, w_in, b_in):
    cin = _dot(hb, w_in[:, QKV_END:CIN_END]) + b_in[:, QKV_END:CIN_END]
    return cin[:, :CONV_WIDTH] * jax.nn.sigmoid(cin[:, CONV_WIDTH:])


def _attn_scores(q_blk, kvh, band_kv, meta_kv):
    heads = range(kvh * GQA_GROUP, (kvh + 1) * GQA_GROUP)
    ks = slice(kvh * HEAD_DIM, (kvh + 1) * HEAD_DIM)
    q = jnp.concatenate([q_blk[:, a * HEAD_DIM:(a + 1) * HEAD_DIM] for a in heads], axis=0)
    s_m = _dot_nt(q, meta_kv[:, ks])
    s_b = None if band_kv is None else _dot_nt(q, band_kv[:, ks])
    return s_m, s_b


def _attn_finish(scores, rows, kvh, band_kv, meta_kv, band_bias, meta_bias, sinks):
    s_m, s_b = scores
    heads = range(kvh * GQA_GROUP, (kvh + 1) * GQA_GROUP)
    vs = slice(KV_WIDTH + kvh * HEAD_DIM, KV_WIDTH + (kvh + 1) * HEAD_DIM)
    sink = jnp.concatenate([jnp.full((rows, 1), sinks[a], F32) for a in heads], axis=0)
    s_m = s_m + meta_bias
    m = jnp.maximum(jnp.max(s_m, axis=-1, keepdims=True), sink)
    if band_kv is not None:
        s_b = s_b + band_bias
        m = jnp.maximum(m, jnp.max(s_b, axis=-1, keepdims=True))
    p_m = jnp.exp(s_m - m)
    den = jnp.sum(p_m, axis=-1, keepdims=True) + jnp.exp(sink - m)
    o = _dot(p_m.astype(BF16), meta_kv[:, vs])
    if band_kv is not None:
        p_b = jnp.exp(s_b - m)
        den = den + jnp.sum(p_b, axis=-1, keepdims=True)
        o = o + _dot(p_b.astype(BF16), band_kv[:, vs])
    o = o / den
    return jnp.concatenate([o[g * rows:(g + 1) * rows] for g in range(GQA_GROUP)], axis=1)


def _attn_scores_t(q_blk, kvh, k_band, k_meta):
    heads = range(kvh * GQA_GROUP, (kvh + 1) * GQA_GROUP)
    ks = slice(kvh * HEAD_DIM, (kvh + 1) * HEAD_DIM)
    q = jnp.concatenate([q_blk[:, a * HEAD_DIM:(a + 1) * HEAD_DIM] for a in heads], axis=0)
    k_all = jnp.concatenate([k_band[:, ks], k_meta[:, ks]], axis=0)
    return _dot_nt(k_all, q)


def _attn_finish_t(s_t, kvh, vt_band, vt_meta, band_bias, meta_bias, sink):
    vrows = slice(kvh * HEAD_DIM, (kvh + 1) * HEAD_DIM)
    cols = s_t.shape[1]
    s_b = s_t[:2 * BLOCK] + band_bias
    s_m = s_t[2 * BLOCK:] + meta_bias
    m = jnp.maximum(jnp.maximum(jnp.max(s_b, axis=0, keepdims=True),
                                jnp.max(s_m, axis=0, keepdims=True)), sink)
    p_b = jnp.exp(s_b - m)
    p_m = jnp.exp(s_m - m)
    den = (jnp.sum(p_b, axis=0, keepdims=True) + jnp.sum(p_m, axis=0, keepdims=True)
           + jnp.exp(sink - m))
    p_m_pad = jnp.concatenate([p_m.astype(BF16), jnp.zeros((BLOCK - N_META, cols), BF16)], axis=0)
    o_t = (_dot(vt_band[vrows], p_b.astype(BF16)) + _dot(vt_meta[vrows], p_m_pad)) / den
    stacked = jnp.concatenate([o_t[:, g * BLOCK:(g + 1) * BLOCK] for g in range(GQA_GROUP)], axis=0)
    return stacked.T


def _conv31_rows(cb, r0, rows, cdw, cdwb):
    first = CONV_HIST - (CONV_TAPS - 1)
    slabs = []
    for s in range(CONV_SLABS):
        ls = slice(s * LANES, (s + 1) * LANES)
        acc = jnp.broadcast_to(cdwb[:, ls], (rows, LANES))
        for k in range(CONV_TAPS):
            acc = acc + cdw[k:k + 1, ls] * cb[s, _tok(r0 + first + k, rows), :]
        slabs.append(acc)
    return jnp.concatenate(slabs, axis=1)


def _conv_branch_rows(cb, r0, rows, cdw, cdwb, clng, clnb):
    cc = _conv31_rows(cb, r0, rows, cdw, cdwb)
    return jax.nn.silu(_layer_norm(cc, clng[...], clnb[...])).astype(BF16)


def _conv3_slab(ub, buf, slab, m_rows, w, b):
    first = FFN_HIST - (FFN_TAPS - 1)
    acc = b
    for k in range(FFN_TAPS):
        acc = acc + w[k:k + 1, :] * ub[buf, slab, _tok(first + k, m_rows), :]
    return acc


def _gate(hb, w_in, b_in, lo):
    return jax.nn.sigmoid(_dot(hb, w_in[:, lo:lo + D_MODEL]) + b_in[:, lo:lo + D_MODEL])


def _mix_out(h, gate_attn, y_attn, gate_conv, y_conv, w_out):
    mixed = (gate_attn * y_attn + gate_conv * y_conv).astype(BF16)
    return DEEPNORM_ALPHA * h + _dot(mixed, w_out[...])


def _ffn(*args):
    result = []
    for _ in _ffn_stages(*args, result.append):
        pass
    return result[0]


def _ffn_stages(m_rows, r1, ln1g, ln1b, w_up, fdw, fdwb, w_down, ln2g, ln2b, ub, upcarry, up_out,
                emit):
    h1 = _layer_norm(r1, ln1g[...], ln1b[...])
    h1b = h1.astype(BF16)
    half = FFN_SLABS // 2
    ahead = FFN_BUFS - 1

    def chunk_cols(j, s):
        base = j * FFN_CHUNK + (s % half) * LANES + (D_FF if s >= half else 0)
        return slice(base, base + LANES)

    def up_proj(j):
        return (_dot(h1b, w_up[:, j * FFN_CHUNK:(j + 1) * FFN_CHUNK]),
                _dot(h1b, w_up[:, D_FF + j * FFN_CHUNK:D_FF + (j + 1) * FFN_CHUNK]))

    acc = jnp.zeros((m_rows, D_MODEL), F32)
    ups = {j: up_proj(j) for j in range(ahead)}
    yield
    for j in range(N_FFN_CHUNKS):
        u, g = ups.pop(j)
        buf = j % FFN_BUFS
        if up_out is not None:
            up_out[:, j * FFN_CHUNK:(j + 1) * FFN_CHUNK] = u
            up_out[:, D_FF + j * FFN_CHUNK:D_FF + (j + 1) * FFN_CHUNK] = g
        for s in range(FFN_SLABS):
            src = u if s < half else g
            ls = slice((s % half) * LANES, (s % half + 1) * LANES)
            ub[buf, s, _tok(0, FFN_HIST), :] = upcarry[j, s]
            ub[buf, s, _tok(FFN_HIST, m_rows), :] = src[:, ls]
            upcarry[j, s] = ub[buf, s, _tok(m_rows, FFN_HIST), :]
        if j + ahead < N_FFN_CHUNKS:
            ups[j + ahead] = up_proj(j + ahead)
        conv = [_conv3_slab(ub, buf, s, m_rows, fdw[:, chunk_cols(j, s)], fdwb[:, chunk_cols(j, s)])
                for s in range(FFN_SLABS)]
        f = jnp.concatenate([_gelu2(conv[half + s].astype(BF16)) * conv[s].astype(BF16)
                             for s in range(half)], axis=1)
        acc = acc + _dot(f, w_down[j * FFN_CHUNK:(j + 1) * FFN_CHUNK, :])
        yield
    emit(_layer_norm(DEEPNORM_ALPHA * h1 + acc, ln2g[...], ln2b[...]))


def _main_kernel(apply_in_ln, n_tiles, tiles_per_seq,
                 h_ref, ilg, ilb, w_in, b_in, sink_t, bband_t, bmeta_t, kmeta, vmeta_t, cmeta, upmeta,
                 w_ap, cdw, cdwb, clng, clnb, w_cp, w_out, ln1g, ln1b,
                 w_up, fdw, fdwb, w_down, ln2g, ln2b,
                 out_ref,
                 q_buf, k_seq, vt_seq, cb, a_buf, h1s, ub, upcarry):
    n = pl.program_id(0)
    t_front = jnp.minimum(n, n_tiles - 1) % tiles_per_seq
    t_back = jnp.maximum(n - 1, 0) % tiles_per_seq
    wslot = n % 2
    rslot = 1 - wslot

    @pl.when(n == 0)
    def _():
        h1s[rslot] = jnp.zeros((TM, D_MODEL), F32)

    @pl.when(t_front == 0)
    def _():
        k_seq[0:BLOCK, :] = jnp.zeros((BLOCK, KV_WIDTH), BF16)
        vt_seq[0] = jnp.zeros((KV_WIDTH, BLOCK), BF16)
        for s in range(CONV_SLABS):
            cb[s, _tok(0, CONV_HIST - N_META), :] = jnp.zeros((CONV_HIST - N_META, LANES), F32)
            cb[s, _tok(CONV_HIST - N_META, N_META), :] = cmeta[:, s * LANES:(s + 1) * LANES]

    @pl.when(t_back == 0)
    def _():
        half = FFN_SLABS // 2
        for j in range(N_FFN_CHUNKS):
            for s in range(FFN_SLABS):
                base = j * FFN_CHUNK + (s % half) * LANES + (D_FF if s >= half else 0)
                upcarry[j, s] = upmeta[N_META - FFN_HIST:N_META, base:base + LANES]

    first = (t_front == 0).astype(jnp.int32)
    blk0 = t_front * (TM // BLOCK)
    k_row0 = pl.multiple_of(blk0 * BLOCK, BLOCK)

    def scores(j, kvh):
        r0 = j * BLOCK
        k_band = k_seq[pl.ds(k_row0 + r0, 2 * BLOCK), :]
        return _attn_scores_t(q_buf[r0:r0 + BLOCK, :], kvh, k_band, kmeta[...])

    def finish(j, kvh, sc):
        r0 = j * BLOCK
        bidx = first if j == 0 else 0
        vt_band = jnp.concatenate([vt_seq[blk0 + j], vt_seq[blk0 + j + 1]], axis=1)
        o = _attn_finish_t(sc, kvh, vt_band, vmeta_t[...], bband_t[bidx, kvh], bmeta_t[bidx, kvh],
                           sink_t[kvh])
        a_buf[r0:r0 + BLOCK, kvh * GROUP_WIDTH:(kvh + 1) * GROUP_WIDTH] = o.astype(BF16)

    def conv_proj(r0):
        cvals = [_conv_branch_rows(cb, r, CONV_ROWS, cdw, cdwb, clng, clnb)
                 for r in range(r0, r0 + BLOCK, CONV_ROWS)]
        return _dot(jnp.concatenate(cvals, axis=0), w_cp[...])

    def mixer_half():
        h = h_ref[0]
        if apply_in_ln:
            h = _layer_norm(h, ilg[...], ilb[...])
        hb = h.astype(BF16)
        qkv = _qkv_proj(hb, w_in, b_in)
        q_buf[...] = (qkv[:, :ATTN_WIDTH] * (HEAD_DIM ** -0.5)).astype(BF16)
        k_seq[pl.ds(k_row0 + BLOCK, TM), :] = qkv[:, ATTN_WIDTH:ATTN_WIDTH + KV_WIDTH].astype(BF16)
        v = qkv[:, ATTN_WIDTH + KV_WIDTH:QKV_END]
        for jb in range(TM // BLOCK):
            vt_seq[blk0 + 1 + jb] = v[jb * BLOCK:(jb + 1) * BLOCK, :].T.astype(BF16)

        c = _glu_proj(hb, w_in, b_in)
        for s in range(CONV_SLABS):
            cb[s, _tok(CONV_HIST, TM), :] = c[:, s * LANES:(s + 1) * LANES]
        yield
        units = [(j, kvh) for j in range(TM // BLOCK) for kvh in range(N_KV_HEADS)]
        assert len(units) == 4
        sc0 = scores(*units[0])
        sc1 = scores(*units[1])
        sc2 = scores(*units[2])
        sc3 = scores(*units[3])
        yield
        gate_attn = _gate(hb, w_in, b_in, CIN_END)
        yield
        gate_conv = _gate(hb, w_in, b_in, CIN_END + D_MODEL)
        yield
        finish(*units[0], sc0)
        yield
        finish(*units[1], sc1)
        y_conv_top = conv_proj(0)
        yield
        finish(*units[2], sc2)
        yield
        finish(*units[3], sc3)
        y_conv_bot = conv_proj(BLOCK)
        yield
        y_attn = _dot(a_buf[...], w_ap[...])
        yield
        y_conv = jnp.concatenate([y_conv_top, y_conv_bot], axis=0)
        h1s[wslot] = _mix_out(h, gate_attn, y_attn, gate_conv, y_conv, w_out)

    def store_out(y):
        out_ref[0] = y

    ffn_half = _ffn_stages(TM, h1s[rslot], ln1g, ln1b, w_up, fdw, fdwb, w_down, ln2g, ln2b,
                           ub, upcarry, None, store_out)
    order = "MFMMMFMFMFMFMFFFFMFFFFM"
    halves = {"M": mixer_half(), "F": ffn_half}
    for which in order:
        next(halves[which], None)
    for gen in halves.values():
        assert next(gen, "done") == "done"

    for s in range(CONV_SLABS):
        cb[s, _tok(0, CONV_HIST), :] = cb[s, _tok(TM, CONV_HIST), :]


def _meta_kernel(apply_in_ln,
                 h_ref, ilg, ilb, w_in, b_in, sinks, bmetaq,
                 w_ap, cdw, cdwb, clng, clnb, w_cp, w_out, ln1g, ln1b,
                 w_up, fdw, fdwb, w_down, ln2g, ln2b,
                 hout_ref, kvpad_ref, cmeta_ref, upmeta_ref,
                 cb, ub, upcarry):
    h = h_ref[...]
    if apply_in_ln:
        h = _layer_norm(h, ilg[...], ilb[...])
    hb = h.astype(BF16)
    qkv = _qkv_proj(hb, w_in, b_in)
    c = _glu_proj(hb, w_in, b_in)
    q = (qkv[:, :ATTN_WIDTH] * (HEAD_DIM ** -0.5)).astype(BF16)
    kvpad_ref[...] = jnp.zeros((BLOCK, 2 * KV_WIDTH), BF16)
    kvpad_ref[0:N_META, :] = qkv[:, ATTN_WIDTH:QKV_END].astype(BF16)
    cmeta_ref[...] = c
    for s in range(CONV_SLABS):
        cb[s, _tok(0, CONV_HIST), :] = jnp.zeros((CONV_HIST, LANES), F32)
        cb[s, _tok(CONV_HIST, N_META), :] = c[:, s * LANES:(s + 1) * LANES]
    upcarry[...] = jnp.zeros(upcarry.shape, F32)

    meta_kv = kvpad_ref[...]
    o = jnp.concatenate(
        [_attn_finish(_attn_scores(q, kvh, None, meta_kv), N_META, kvh, None, meta_kv, None,
                      bmetaq[kvh * GQA_GROUP:(kvh + 1) * GQA_GROUP].reshape(GQA_GROUP * N_META, BLOCK),
                      sinks)
         for kvh in range(N_KV_HEADS)], axis=1)
    conv_b = _conv_branch_rows(cb, 0, N_META, cdw, cdwb, clng, clnb)
    y_attn = _dot(o.astype(BF16), w_ap[...])
    y_conv = _dot(conv_b, w_cp[...])
    r1 = _mix_out(h, _gate(hb, w_in, b_in, CIN_END), y_attn,
                  _gate(hb, w_in, b_in, CIN_END + D_MODEL), y_conv, w_out)
    hout_ref[...] = _ffn(N_META, r1, ln1g, ln1b, w_up, fdw, fdwb, w_down, ln2g, ln2b, ub, upcarry,
                         upmeta_ref)


def _t5_bucket(d):
    n = jnp.maximum(d, 0)
    max_exact = REL_BUCKETS // 2
    nf = jnp.maximum(n, 1).astype(F32)
    large = max_exact + (jnp.log(nf / max_exact) / math.log(REL_MAX_DIST / max_exact)
                         * (REL_BUCKETS - max_exact)).astype(jnp.int32)
    large = jnp.minimum(large, REL_BUCKETS - 1)
    return jnp.where(n < max_exact, n, large)


def _bias_tables(rel_bias):
    rb = rel_bias.astype(F32)

    def bias_of(d, valid):
        onehot = jax.nn.one_hot(_t5_bucket(d), REL_BUCKETS, dtype=F32)
        b = jnp.einsum("qkb,bh->hqk", onehot, rb, precision=lax.Precision.HIGHEST)
        return jnp.where(valid[None], b, MASKED)

    i = jnp.arange(BLOCK)[:, None]
    j = jnp.arange(2 * BLOCK)[None, :]
    d = i + BLOCK - j
    valid = (d >= 0) & (d < WINDOW)
    band = jnp.stack([bias_of(d, valid), bias_of(d, valid & (j >= BLOCK))])
    far = N_META + BLOCK - (N_META - 1)
    assert 16 + int(math.log(far / 16) / math.log(REL_MAX_DIST / 16) * 16) >= REL_BUCKETS - 1
    mcol = jnp.arange(BLOCK)[None, :]
    mvalid = jnp.broadcast_to(mcol < N_META, (BLOCK, BLOCK))
    meta = jnp.stack([bias_of(N_META + BLOCK + i - mcol, mvalid),
                      bias_of(N_META + i - mcol, mvalid)])
    qi = jnp.arange(N_META)[:, None]
    metaq = bias_of(qi - mcol, (mcol <= qi) & (mcol < N_META))

    def keys_on_rows(tbl, n_keys):
        t = tbl[..., :n_keys].reshape(2, N_KV_HEADS, GQA_GROUP, BLOCK, n_keys)
        return t.transpose(0, 1, 4, 2, 3).reshape(2, N_KV_HEADS, n_keys, GQA_GROUP * BLOCK)

    return keys_on_rows(band, 2 * BLOCK), keys_on_rows(meta, N_META), metaq


def _vmem_spec():
    return pl.BlockSpec(memory_space=pltpu.VMEM)


def _conv_scratch(m_rows):
    return [
        pltpu.VMEM((CONV_SLABS, 2 * (CONV_HIST + m_rows), LANES), F32),
        pltpu.VMEM((FFN_BUFS, FFN_SLABS, 2 * (FFN_HIST + m_rows), LANES), F32),
        pltpu.VMEM((N_FFN_CHUNKS, FFN_SLABS, FFN_HIST, LANES), F32),
    ]


def _meta_layer(apply_in_ln, hm, ilg, ilb, lw, bmetaq):
    n_up = 2 * D_FF
    out_shape = (jax.ShapeDtypeStruct((N_META, D_MODEL), F32),
                 jax.ShapeDtypeStruct((BLOCK, 2 * KV_WIDTH), BF16),
                 jax.ShapeDtypeStruct((N_META, CONV_WIDTH), F32),
                 jax.ShapeDtypeStruct((N_META, n_up), F32))
    ins = (hm, ilg, ilb, lw["w_in"], lw["b_in"], lw["sinks"], bmetaq,
           lw["w_ap"], lw["cdw"], lw["cdwb"], lw["clng"], lw["clnb"], lw["w_cp"], lw["w_out"],
           lw["ln1g"], lw["ln1b"], lw["w_up"], lw["fdw"], lw["fdwb"], lw["w_down"],
           lw["ln2g"], lw["ln2b"])
    in_specs = [_vmem_spec() for _ in ins]
    in_specs[5] = pl.BlockSpec(memory_space=pltpu.SMEM)
    return pl.pallas_call(
        functools.partial(_meta_kernel, apply_in_ln),
        out_shape=out_shape,
        in_specs=in_specs,
        out_specs=tuple(_vmem_spec() for _ in out_shape),
        scratch_shapes=_conv_scratch(N_META),
        compiler_params=pltpu.CompilerParams(vmem_limit_bytes=VMEM_LIMIT_BYTES),
        name="meta_layer",
    )(*ins)


def _main_layer(apply_in_ln, h, ilg, ilb, lw, bband_t, bmeta_t, kvmeta, cmeta, upmeta):
    batch, seq, _ = h.shape
    assert seq % TM == 0 and TM % BLOCK == 0 and TM % CONV_ROWS == 0
    sink_t = jnp.repeat(lw["sinks"].reshape(N_KV_HEADS, 1, GQA_GROUP), BLOCK, axis=2)
    kmeta = kvmeta[:N_META, :KV_WIDTH]
    vmeta_t = kvmeta[:, KV_WIDTH:].T
    ins = (h, ilg, ilb, lw["w_in"], lw["b_in"], sink_t, bband_t, bmeta_t, kmeta, vmeta_t, cmeta, upmeta,
           lw["w_ap"], lw["cdw"], lw["cdwb"], lw["clng"], lw["clnb"], lw["w_cp"], lw["w_out"],
           lw["ln1g"], lw["ln1b"], lw["w_up"], lw["fdw"], lw["fdwb"], lw["w_down"],
           lw["ln2g"], lw["ln2b"])
    tiles_per_seq = seq // TM
    n_tiles = batch * tiles_per_seq

    def tile_of(n):
        return (n // tiles_per_seq, n % tiles_per_seq, 0)

    in_tile = pl.BlockSpec((1, TM, D_MODEL), lambda n: tile_of(jnp.minimum(n, n_tiles - 1)))
    out_tile = pl.BlockSpec((1, TM, D_MODEL), lambda n: tile_of(jnp.maximum(n - 1, 0)))
    in_specs = [_vmem_spec() for _ in ins]
    in_specs[0] = in_tile
    return pl.pallas_call(
        functools.partial(_main_kernel, apply_in_ln, n_tiles, tiles_per_seq),
        out_shape=jax.ShapeDtypeStruct((batch, seq, D_MODEL), F32),
        grid=(n_tiles + 1,),
        in_specs=in_specs,
        out_specs=out_tile,
        scratch_shapes=[
            pltpu.VMEM((TM, ATTN_WIDTH), BF16),
            pltpu.VMEM((BLOCK + seq, KV_WIDTH), BF16),
            pltpu.VMEM((1 + seq // BLOCK, KV_WIDTH, BLOCK), BF16),
        ] + _conv_scratch(TM)[:1] + [
            pltpu.VMEM((TM, ATTN_WIDTH), BF16),
            pltpu.VMEM((2, TM, D_MODEL), F32),
        ] + _conv_scratch(TM)[1:],
        compiler_params=pltpu.CompilerParams(
            dimension_semantics=("arbitrary",),
            vmem_limit_bytes=VMEM_LIMIT_BYTES),
        name="main_layer",
    )(*ins)


def _layer_weights(i, w_in, b_in, attn_sinks, w_attn_proj, conv_dw, conv_dw_b, conv_ln_g,
                   conv_ln_b, w_conv_proj, w_out, ln1_g, ln1_b, ffn_w_up, ffn_dw, ffn_dw_b,
                   ffn_w_down, ln2_g, ln2_b):
    row = lambda v: v[i][None, :].astype(F32)
    return dict(
        w_in=w_in[i].astype(BF16), b_in=row(b_in), sinks=attn_sinks[i].astype(F32),
        w_ap=w_attn_proj[i].astype(BF16), cdw=conv_dw[i].astype(F32), cdwb=row(conv_dw_b),
        clng=row(conv_ln_g), clnb=row(conv_ln_b), w_cp=w_conv_proj[i].astype(BF16),
        w_out=w_out[i].astype(BF16), ln1g=row(ln1_g), ln1b=row(ln1_b),
        w_up=ffn_w_up[i].astype(BF16), fdw=ffn_dw[i].astype(F32), fdwb=row(ffn_dw_b),
        w_down=(0.5 * ffn_w_down[i]).astype(BF16), ln2g=row(ln2_g), ln2b=row(ln2_b))


def kernel(x, meta_tokens, in_ln_g, in_ln_b, rel_bias, w_in, b_in, attn_sinks, w_attn_proj, conv_dw, conv_dw_b, conv_ln_g, conv_ln_b, w_conv_proj, w_out, ln1_g, ln1_b, ffn_w_up, ffn_dw, ffn_dw_b, ffn_w_down, ln2_g, ln2_b):
    bband, bmeta, bmetaq = _bias_tables(rel_bias)
    ilg = in_ln_g[None, :].astype(F32)
    ilb = in_ln_b[None, :].astype(F32)
    h = x.astype(F32)
    hm = meta_tokens.astype(F32)
    for i in range(DEPTH):
        lw = _layer_weights(i, w_in, b_in, attn_sinks, w_attn_proj, conv_dw, conv_dw_b,
                            conv_ln_g, conv_ln_b, w_conv_proj, w_out, ln1_g, ln1_b, ffn_w_up,
                            ffn_dw, ffn_dw_b, ffn_w_down, ln2_g, ln2_b)
        hm_next, kvmeta, cmeta, upmeta = _meta_layer(i == 0, hm, ilg, ilb, lw, bmetaq)
        h = _main_layer(i == 0, h, ilg, ilb, lw, bband, bmeta, kvmeta, cmeta, upmeta)
        hm = hm_next
    return h
```

```python
import functools
import math

import jax
import jax.numpy as jnp
from jax import lax
from jax.experimental import pallas as pl
from jax.experimental.pallas import tpu as pltpu

D_MODEL = 1024
DEPTH = 2
N_META = 16
HEAD_DIM = 64
N_Q_HEADS = 8
N_KV_HEADS = 2
GQA_GROUP = N_Q_HEADS // N_KV_HEADS
ATTN_WIDTH = N_Q_HEADS * HEAD_DIM
KV_WIDTH = N_KV_HEADS * HEAD_DIM
GROUP_WIDTH = GQA_GROUP * HEAD_DIM
WINDOW = 128
BLOCK = 128
CONV_WIDTH = D_MODEL // 2
CONV_TAPS = 31
REL_BUCKETS = 32
REL_MAX_DIST = 128
D_FF = 2816
FFN_TAPS = 3
LN_EPS = 1e-5
DEEPNORM_ALPHA = (2.0 * DEPTH) ** 0.25
IN_COLS = ATTN_WIDTH + 2 * KV_WIDTH + 2 * CONV_WIDTH + 2 * D_MODEL

QKV_END = ATTN_WIDTH + 2 * KV_WIDTH
CIN_END = QKV_END + 2 * CONV_WIDTH

LANES = 128
TM = 256
FFN_CHUNK = 256
N_FFN_CHUNKS = D_FF // FFN_CHUNK
FFN_SLABS = 2 * FFN_CHUNK // LANES
FFN_BUFS = 3
CONV_SLABS = CONV_WIDTH // LANES
CONV_HIST = 32
FFN_HIST = 8
CONV_ROWS = 64
MASKED = -1e30
LOG2E = math.log2(math.e)
VMEM_LIMIT_BYTES = 60000 * 1024

BF16 = jnp.bfloat16
F32 = jnp.float32


def _dot(a, b):
    return jnp.dot(a, b, preferred_element_type=F32)


def _dot_nt(a, b):
    return lax.dot_general(a, b, (((1,), (1,)), ((), ())), preferred_element_type=F32)


def _layer_norm(x, g, b):
    mu = jnp.mean(x, axis=-1, keepdims=True)
    xc = x - mu
    var = jnp.mean(xc * xc, axis=-1, keepdims=True)
    return xc * lax.rsqrt(var + LN_EPS) * g + b


def _gelu2(x):
    one = jnp.asarray(1.0, x.dtype)
    return x * (one + lax.erf(x * jnp.asarray(math.sqrt(0.5), x.dtype)))


def _tok(start, n):
    return pl.ds(2 * start, n, stride=2)


def _qkv_proj(hb, w_in, b_in):
    return _dot(hb, w_in[:, 0:QKV_END]) + b_in[:, 0:QKV_END]


def _glu_proj(hb, w_in, b_in):
    cin = _dot(hb, w_in[:, QKV_END:CIN_END]) + b_in[:, QKV_END:CIN_END]
    return cin[:, :CONV_WIDTH] * jax.nn.sigmoid(cin[:, CONV_WIDTH:])


def _attn_scores(q_blk, kvh, band_kv, meta_kv):
    heads = range(kvh * GQA_GROUP, (kvh + 1) * GQA_GROUP)
    ks = slice(kvh * HEAD_DIM, (kvh + 1) * HEAD_DIM)
    q = jnp.concatenate([q_blk[:, a * HEAD_DIM:(a + 1) * HEAD_DIM] for a in heads], axis=0)
    s_m = _dot_nt(q, meta_kv[:, ks])
    s_b = None if band_kv is None else _dot_nt(q, band_kv[:, ks])
    return s_m, s_b


def _attn_finish(scores, rows, kvh, band_kv, meta_kv, band_bias, meta_bias, sinks):
    s_m, s_b = scores
    heads = range(kvh * GQA_GROUP, (kvh + 1) * GQA_GROUP)
    vs = slice(KV_WIDTH + kvh * HEAD_DIM, KV_WIDTH + (kvh + 1) * HEAD_DIM)
    sink = jnp.concatenate([jnp.full((rows, 1), sinks[a], F32) for a in heads], axis=0)
    s_m = s_m + meta_bias
    m = jnp.maximum(jnp.max(s_m, axis=-1, keepdims=True), sink)
    if band_kv is not None:
        s_b = s_b + band_bias
        m = jnp.maximum(m, jnp.max(s_b, axis=-1, keepdims=True))
    p_m = jnp.exp(s_m - m)
    den = jnp.sum(p_m, axis=-1, keepdims=True) + jnp.exp(sink - m)
    o = _dot(p_m.astype(BF16), meta_kv[:, vs])
    if band_kv is not None:
        p_b = jnp.exp(s_b - m)
        den = den + jnp.sum(p_b, axis=-1, keepdims=True)
        o = o + _dot(p_b.astype(BF16), band_kv[:, vs])
    o = o / den
    return jnp.concatenate([o[g * rows:(g + 1) * rows] for g in range(GQA_GROUP)], axis=1)


def _attn_scores_t(q_blk, kvh, k_band, k_meta):
    heads = range(kvh * GQA_GROUP, (kvh + 1) * GQA_GROUP)
    ks = slice(kvh * HEAD_DIM, (kvh + 1) * HEAD_DIM)
    q = jnp.concatenate([q_blk[:, a * HEAD_DIM:(a + 1) * HEAD_DIM] for a in heads], axis=0)
    k_all = jnp.concatenate([k_band[:, ks], k_meta[:, ks]], axis=0)
    return _dot_nt(k_all, q)


def _attn_finish_t(s_t, kvh, vt_band, vt_meta, band_bias, meta_bias, sink):
    vrows = slice(kvh * HEAD_DIM, (kvh + 1) * HEAD_DIM)
    cols = s_t.shape[1]
    s_b = s_t[:2 * BLOCK] + band_bias
    s_m = s_t[2 * BLOCK:] + meta_bias
    m = jnp.maximum(jnp.maximum(jnp.max(s_b, axis=0, keepdims=True),
                                jnp.max(s_m, axis=0, keepdims=True)), sink)
    p_b = jnp.exp2(s_b - m)
    p_m = jnp.exp2(s_m - m)
    den = (jnp.sum(p_b, axis=0, keepdims=True) + jnp.sum(p_m, axis=0, keepdims=True)
           + jnp.exp2(sink - m))
    p_m_pad = jnp.concatenate([p_m.astype(BF16), jnp.zeros((BLOCK - N_META, cols), BF16)], axis=0)
    o_t = (_dot(vt_band[vrows], p_b.astype(BF16)) + _dot(vt_meta[vrows], p_m_pad)) * (1.0 / den)
    stacked = jnp.concatenate([o_t[:, g * BLOCK:(g + 1) * BLOCK] for g in range(GQA_GROUP)], axis=0)
    return stacked.T


def _conv31_rows(cb, r0, rows, cdw, cdwb):
    first = CONV_HIST - (CONV_TAPS - 1)
    slabs = []
    for s in range(CONV_SLABS):
        ls = slice(s * LANES, (s + 1) * LANES)
        acc = jnp.broadcast_to(cdwb[:, ls], (rows, LANES))
        for k in range(CONV_TAPS):
            acc = acc + cdw[k:k + 1, ls] * cb[s, _tok(r0 + first + k, rows), :]
        slabs.append(acc)
    return jnp.concatenate(slabs, axis=1)


def _conv_branch_rows(cb, r0, rows, cdw, cdwb, clng, clnb):
    cc = _conv31_rows(cb, r0, rows, cdw, cdwb)
    return jax.nn.silu(_layer_norm(cc, clng[...], clnb[...])).astype(BF16)


def _conv3_slab(ub, buf, slab, m_rows, w, b):
    first = FFN_HIST - (FFN_TAPS - 1)
    acc = b
    for k in range(FFN_TAPS):
        acc = acc + w[k:k + 1, :] * ub[buf, slab, _tok(first + k, m_rows), :]
    return acc


def _gate(hb, w_in, b_in, lo):
    return jax.nn.sigmoid(_dot(hb, w_in[:, lo:lo + D_MODEL]) + b_in[:, lo:lo + D_MODEL])


def _mix_out(h, gate_attn, y_attn, gate_conv, y_conv, w_out):
    mixed = (gate_attn * y_attn + gate_conv * y_conv).astype(BF16)
    return DEEPNORM_ALPHA * h + _dot(mixed, w_out[...])


def _ffn(*args):
    result = []
    for _ in _ffn_stages(*args, result.append):
        pass
    return result[0]


def _ffn_stages(m_rows, r1, ln1g, ln1b, w_up, fdw, fdwb, w_down, ln2g, ln2b, ub, upcarry, up_out,
                emit):
    h1 = _layer_norm(r1, ln1g[...], ln1b[...])
    h1b = h1.astype(BF16)
    half = FFN_SLABS // 2
    ahead = FFN_BUFS - 1

    def chunk_cols(j, s):
        base = j * FFN_CHUNK + (s % half) * LANES + (D_FF if s >= half else 0)
        return slice(base, base + LANES)

    def up_proj(j):
        return (_dot(h1b, w_up[:, j * FFN_CHUNK:(j + 1) * FFN_CHUNK]),
                _dot(h1b, w_up[:, D_FF + j * FFN_CHUNK:D_FF + (j + 1) * FFN_CHUNK]))

    acc = jnp.zeros((m_rows, D_MODEL), F32)
    ups = {j: up_proj(j) for j in range(ahead)}
    yield
    for j in range(N_FFN_CHUNKS):
        u, g = ups.pop(j)
        buf = j % FFN_BUFS
        if up_out is not None:
            up_out[:, j * FFN_CHUNK:(j + 1) * FFN_CHUNK] = u
            up_out[:, D_FF + j * FFN_CHUNK:D_FF + (j + 1) * FFN_CHUNK] = g
        for s in range(FFN_SLABS):
            src = u if s < half else g
            ls = slice((s % half) * LANES, (s % half + 1) * LANES)
            ub[buf, s, _tok(0, FFN_HIST), :] = upcarry[j, s]
            ub[buf, s, _tok(FFN_HIST, m_rows), :] = src[:, ls]
            upcarry[j, s] = ub[buf, s, _tok(m_rows, FFN_HIST), :]
        if j + ahead < N_FFN_CHUNKS:
            ups[j + ahead] = up_proj(j + ahead)
        conv = [_conv3_slab(ub, buf, s, m_rows, fdw[:, chunk_cols(j, s)], fdwb[:, chunk_cols(j, s)])
                for s in range(FFN_SLABS)]
        f = jnp.concatenate([_gelu2(conv[half + s].astype(BF16)) * conv[s].astype(BF16)
                             for s in range(half)], axis=1)
        acc = acc + _dot(f, w_down[j * FFN_CHUNK:(j + 1) * FFN_CHUNK, :])
        yield
    emit(_layer_norm(DEEPNORM_ALPHA * h1 + acc, ln2g[...], ln2b[...]))


def _main_kernel(apply_in_ln, n_tiles, tiles_per_seq,
                 h_ref, ilg, ilb, w_in, b_in, sink_t, bband_t, bmeta_t, kmeta, vmeta_t, cmeta, upmeta,
                 w_ap, cdw, cdwb, clng, clnb, w_cp, w_out, ln1g, ln1b,
                 w_up, fdw, fdwb, w_down, ln2g, ln2b,
                 out_ref,
                 q_buf, k_seq, vt_seq, cb, a_buf, h1s, ub, upcarry):
    n = pl.program_id(0)
    t_front = jnp.minimum(n, n_tiles - 1) % tiles_per_seq
    t_back = jnp.maximum(n - 1, 0) % tiles_per_seq
    wslot = n % 2
    rslot = 1 - wslot

    @pl.when(n == 0)
    def _():
        h1s[rslot] = jnp.zeros((TM, D_MODEL), F32)

    @pl.when(t_front == 0)
    def _():
        k_seq[0:BLOCK, :] = jnp.zeros((BLOCK, KV_WIDTH), BF16)
        vt_seq[0] = jnp.zeros((KV_WIDTH, BLOCK), BF16)
        for s in range(CONV_SLABS):
            cb[s, _tok(0, CONV_HIST - N_META), :] = jnp.zeros((CONV_HIST - N_META, LANES), F32)
            cb[s, _tok(CONV_HIST - N_META, N_META), :] = cmeta[:, s * LANES:(s + 1) * LANES]

    @pl.when(t_back == 0)
    def _():
        half = FFN_SLABS // 2
        for j in range(N_FFN_CHUNKS):
            for s in range(FFN_SLABS):
                base = j * FFN_CHUNK + (s % half) * LANES + (D_FF if s >= half else 0)
                upcarry[j, s] = upmeta[N_META - FFN_HIST:N_META, base:base + LANES]

    first = (t_front == 0).astype(jnp.int32)
    blk0 = t_front * (TM // BLOCK)
    k_row0 = pl.multiple_of(blk0 * BLOCK, BLOCK)

    def scores(j, kvh):
        r0 = j * BLOCK
        k_band = k_seq[pl.ds(k_row0 + r0, 2 * BLOCK), :]
        return _attn_scores_t(q_buf[r0:r0 + BLOCK, :], kvh, k_band, kmeta[...])

    def finish(j, kvh, sc):
        r0 = j * BLOCK
        bidx = first if j == 0 else 0
        vt_band = jnp.concatenate([vt_seq[blk0 + j], vt_seq[blk0 + j + 1]], axis=1)
        o = _attn_finish_t(sc, kvh, vt_band, vmeta_t[...], bband_t[bidx, kvh], bmeta_t[bidx, kvh],
                           sink_t[kvh])
        a_buf[r0:r0 + BLOCK, kvh * GROUP_WIDTH:(kvh + 1) * GROUP_WIDTH] = o.astype(BF16)

    def conv_proj(r0):
        cvals = [_conv_branch_rows(cb, r, CONV_ROWS, cdw, cdwb, clng, clnb)
                 for r in range(r0, r0 + BLOCK, CONV_ROWS)]
        return _dot(jnp.concatenate(cvals, axis=0), w_cp[...])

    def mixer_half():
        h = h_ref[0]
        if apply_in_ln:
            h = _layer_norm(h, ilg[...], ilb[...])
        hb = h.astype(BF16)
        qkv = _qkv_proj(hb, w_in, b_in)
        q_buf[...] = (qkv[:, :ATTN_WIDTH] * (HEAD_DIM ** -0.5 * LOG2E)).astype(BF16)
        k_seq[pl.ds(k_row0 + BLOCK, TM), :] = qkv[:, ATTN_WIDTH:ATTN_WIDTH + KV_WIDTH].astype(BF16)
        v = qkv[:, ATTN_WIDTH + KV_WIDTH:QKV_END]
        for jb in range(TM // BLOCK):
            vt_seq[blk0 + 1 + jb] = v[jb * BLOCK:(jb + 1) * BLOCK, :].T.astype(BF16)

        c = _glu_proj(hb, w_in, b_in)
        for s in range(CONV_SLABS):
            cb[s, _tok(CONV_HIST, TM), :] = c[:, s * LANES:(s + 1) * LANES]
        yield
        units = [(j, kvh) for j in range(TM // BLOCK) for kvh in range(N_KV_HEADS)]
        assert len(units) == 4
        sc0 = scores(*units[0])
        sc1 = scores(*units[1])
        sc2 = scores(*units[2])
        sc3 = scores(*units[3])
        yield
        gate_attn = _gate(hb, w_in, b_in, CIN_END)
        yield
        gate_conv = _gate(hb, w_in, b_in, CIN_END + D_MODEL)
        yield
        finish(*units[0], sc0)
        yield
        finish(*units[1], sc1)
        y_conv_top = conv_proj(0)
        yield
        finish(*units[2], sc2)
        yield
        finish(*units[3], sc3)
        y_conv_bot = conv_proj(BLOCK)
        yield
        y_attn = _dot(a_buf[...], w_ap[...])
        yield
        y_conv = jnp.concatenate([y_conv_top, y_conv_bot], axis=0)
        h1s[wslot] = _mix_out(h, gate_attn, y_attn, gate_conv, y_conv, w_out)

    def store_out(y):
        out_ref[0] = y

    ffn_half = _ffn_stages(TM, h1s[rslot], ln1g, ln1b, w_up, fdw, fdwb, w_down, ln2g, ln2b,
                           ub, upcarry, None, store_out)
    order = "MFMMMFMFMFMFMFFFFMFFFFM"
    halves = {"M": mixer_half(), "F": ffn_half}
    for which in order:
        next(halves[which], None)
    for gen in halves.values():
        assert next(gen, "done") == "done"

    for s in range(CONV_SLABS):
        cb[s, _tok(0, CONV_HIST), :] = cb[s, _tok(TM, CONV_HIST), :]


def _meta_kernel(apply_in_ln,
                 h_ref, ilg, ilb, w_in, b_in, sinks, bmetaq,
                 w_ap, cdw, cdwb, clng, clnb, w_cp, w_out, ln1g, ln1b,
                 w_up, fdw, fdwb, w_down, ln2g, ln2b,
                 hout_ref, kvpad_ref, cmeta_ref, upmeta_ref,
                 cb, ub, upcarry):
    h = h_ref[...]
    if apply_in_ln:
        h = _layer_norm(h, ilg[...], ilb[...])
    hb = h.astype(BF16)
    qkv = _qkv_proj(hb, w_in, b_in)
    c = _glu_proj(hb, w_in, b_in)
    q = (qkv[:, :ATTN_WIDTH] * (HEAD_DIM ** -0.5)).astype(BF16)
    kvpad_ref[...] = jnp.zeros((BLOCK, 2 * KV_WIDTH), BF16)
    kvpad_ref[0:N_META, :] = qkv[:, ATTN_WIDTH:QKV_END].astype(BF16)
    cmeta_ref[...] = c
    for s in range(CONV_SLABS):
        cb[s, _tok(0, CONV_HIST), :] = jnp.zeros((CONV_HIST, LANES), F32)
        cb[s, _tok(CONV_HIST, N_META), :] = c[:, s * LANES:(s + 1) * LANES]
    upcarry[...] = jnp.zeros(upcarry.shape, F32)

    meta_kv = kvpad_ref[...]
    o = jnp.concatenate(
        [_attn_finish(_attn_scores(q, kvh, None, meta_kv), N_META, kvh, None, meta_kv, None,
                      bmetaq[kvh * GQA_GROUP:(kvh + 1) * GQA_GROUP].reshape(GQA_GROUP * N_META, BLOCK),
                      sinks)
         for kvh in range(N_KV_HEADS)], axis=1)
    conv_b = _conv_branch_rows(cb, 0, N_META, cdw, cdwb, clng, clnb)
    y_attn = _dot(o.astype(BF16), w_ap[...])
    y_conv = _dot(conv_b, w_cp[...])
    r1 = _mix_out(h, _gate(hb, w_in, b_in, CIN_END), y_attn,
                  _gate(hb, w_in, b_in, CIN_END + D_MODEL), y_conv, w_out)
    hout_ref[...] = _ffn(N_META, r1, ln1g, ln1b, w_up, fdw, fdwb, w_down, ln2g, ln2b, ub, upcarry,
                         upmeta_ref)


def _t5_bucket(d):
    n = jnp.maximum(d, 0)
    max_exact = REL_BUCKETS // 2
    nf = jnp.maximum(n, 1).astype(F32)
    large = max_exact + (jnp.log(nf / max_exact) / math.log(REL_MAX_DIST / max_exact)
                         * (REL_BUCKETS - max_exact)).astype(jnp.int32)
    large = jnp.minimum(large, REL_BUCKETS - 1)
    return jnp.where(n < max_exact, n, large)


def _bias_tables(rel_bias):
    rb = rel_bias.astype(F32)

    def bias_of(d, valid):
        onehot = jax.nn.one_hot(_t5_bucket(d), REL_BUCKETS, dtype=F32)
        b = jnp.einsum("qkb,bh->hqk", onehot, rb, precision=lax.Precision.HIGHEST)
        return jnp.where(valid[None], b, MASKED)

    i = jnp.arange(BLOCK)[:, None]
    j = jnp.arange(2 * BLOCK)[None, :]
    d = i + BLOCK - j
    valid = (d >= 0) & (d < WINDOW)
    band = jnp.stack([bias_of(d, valid), bias_of(d, valid & (j >= BLOCK))])
    far = N_META + BLOCK - (N_META - 1)
    assert 16 + int(math.log(far / 16) / math.log(REL_MAX_DIST / 16) * 16) >= REL_BUCKETS - 1
    mcol = jnp.arange(BLOCK)[None, :]
    mvalid = jnp.broadcast_to(mcol < N_META, (BLOCK, BLOCK))
    meta = jnp.stack([bias_of(N_META + BLOCK + i - mcol, mvalid),
                      bias_of(N_META + i - mcol, mvalid)])
    qi = jnp.arange(N_META)[:, None]
    metaq = bias_of(qi - mcol, (mcol <= qi) & (mcol < N_META))

    def keys_on_rows(tbl, n_keys):
        t = tbl[..., :n_keys].reshape(2, N_KV_HEADS, GQA_GROUP, BLOCK, n_keys)
        return t.transpose(0, 1, 4, 2, 3).reshape(2, N_KV_HEADS, n_keys, GQA_GROUP * BLOCK)

    return LOG2E * keys_on_rows(band, 2 * BLOCK), LOG2E * keys_on_rows(meta, N_META), metaq


def _vmem_spec():
    return pl.BlockSpec(memory_space=pltpu.VMEM)


def _conv_scratch(m_rows):
    return [
        pltpu.VMEM((CONV_SLABS, 2 * (CONV_HIST + m_rows), LANES), F32),
        pltpu.VMEM((FFN_BUFS, FFN_SLABS, 2 * (FFN_HIST + m_rows), LANES), F32),
        pltpu.VMEM((N_FFN_CHUNKS, FFN_SLABS, FFN_HIST, LANES), F32),
    ]


def _meta_layer(apply_in_ln, hm, ilg, ilb, lw, bmetaq):
    n_up = 2 * D_FF
    out_shape = (jax.ShapeDtypeStruct((N_META, D_MODEL), F32),
                 jax.ShapeDtypeStruct((BLOCK, 2 * KV_WIDTH), BF16),
                 jax.ShapeDtypeStruct((N_META, CONV_WIDTH), F32),
                 jax.ShapeDtypeStruct((N_META, n_up), F32))
    ins = (hm, ilg, ilb, lw["w_in"], lw["b_in"], lw["sinks"], bmetaq,
           lw["w_ap"], lw["cdw"], lw["cdwb"], lw["clng"], lw["clnb"], lw["w_cp"], lw["w_out"],
           lw["ln1g"], lw["ln1b"], lw["w_up"], lw["fdw"], lw["fdwb"], lw["w_down"],
           lw["ln2g"], lw["ln2b"])
    in_specs = [_vmem_spec() for _ in ins]
    in_specs[5] = pl.BlockSpec(memory_space=pltpu.SMEM)
    return pl.pallas_call(
        functools.partial(_meta_kernel, apply_in_ln),
        out_shape=out_shape,
        in_specs=in_specs,
        out_specs=tuple(_vmem_spec() for _ in out_shape),
        scratch_shapes=_conv_scratch(N_META),
        compiler_params=pltpu.CompilerParams(vmem_limit_bytes=VMEM_LIMIT_BYTES),
        name="meta_layer",
    )(*ins)


def _main_layer(apply_in_ln, h, ilg, ilb, lw, bband_t, bmeta_t, kvmeta, cmeta, upmeta):
    batch, seq, _ = h.shape
    assert seq % TM == 0 and TM % BLOCK == 0 and TM % CONV_ROWS == 0
    sink_t = LOG2E * jnp.repeat(lw["sinks"].reshape(N_KV_HEADS, 1, GQA_GROUP), BLOCK, axis=2)
    kmeta = kvmeta[:N_META, :KV_WIDTH]
    vmeta_t = kvmeta[:, KV_WIDTH:].T
    ins = (h, ilg, ilb, lw["w_in"], lw["b_in"], sink_t, bband_t, bmeta_t, kmeta, vmeta_t, cmeta, upmeta,
           lw["w_ap"], lw["cdw"], lw["cdwb"], lw["clng"], lw["clnb"], lw["w_cp"], lw["w_out"],
           lw["ln1g"], lw["ln1b"], lw["w_up"], lw["fdw"], lw["fdwb"], lw["w_down"],
           lw["ln2g"], lw["ln2b"])
    tiles_per_seq = seq // TM
    n_tiles = batch * tiles_per_seq

    def tile_of(n):
        return (n // tiles_per_seq, n % tiles_per_seq, 0)

    in_tile = pl.BlockSpec((1, TM, D_MODEL), lambda n: tile_of(jnp.minimum(n, n_tiles - 1)))
    out_tile = pl.BlockSpec((1, TM, D_MODEL), lambda n: tile_of(jnp.maximum(n - 1, 0)))
    in_specs = [_vmem_spec() for _ in ins]
    in_specs[0] = in_tile
    return pl.pallas_call(
        functools.partial(_main_kernel, apply_in_ln, n_tiles, tiles_per_seq),
        out_shape=jax.ShapeDtypeStruct((batch, seq, D_MODEL), F32),
        grid=(n_tiles + 1,),
        in_specs=in_specs,
        out_specs=out_tile,
        scratch_shapes=[
            pltpu.VMEM((TM, ATTN_WIDTH), BF16),
            pltpu.VMEM((BLOCK + seq, KV_WIDTH), BF16),
            pltpu.VMEM((1 + seq // BLOCK, KV_WIDTH, BLOCK), BF16),
        ] + _conv_scratch(TM)[:1] + [
            pltpu.VMEM((TM, ATTN_WIDTH), BF16),
            pltpu.VMEM((2, TM, D_MODEL), F32),
        ] + _conv_scratch(TM)[1:],
        compiler_params=pltpu.CompilerParams(
            dimension_semantics=("arbitrary",),
            vmem_limit_bytes=VMEM_LIMIT_BYTES),
        name="main_layer",
    )(*ins)


def _layer_weights(i, w_in, b_in, attn_sinks, w_attn_proj, conv_dw, conv_dw_b, conv_ln_g,
                   conv_ln_b, w_conv_proj, w_out, ln1_g, ln1_b, ffn_w_up, ffn_dw, ffn_dw_b,
                   ffn_w_down, ln2_g, ln2_b):
    row = lambda v: v[i][None, :].astype(F32)
    return dict(
        w_in=w_in[i].astype(BF16), b_in=row(b_in), sinks=attn_sinks[i].astype(F32),
        w_ap=w_attn_proj[i].astype(BF16), cdw=conv_dw[i].astype(F32), cdwb=row(conv_dw_b),
        clng=row(conv_ln_g), clnb=row(conv_ln_b), w_cp=w_conv_proj[i].astype(BF16),
        w_out=w_out[i].astype(BF16), ln1g=row(ln1_g), ln1b=row(ln1_b),
        w_up=ffn_w_up[i].astype(BF16), fdw=ffn_dw[i].astype(F32), fdwb=row(ffn_dw_b),
        w_down=(0.5 * ffn_w_down[i]).astype(BF16), ln2g=row(ln2_g), ln2b=row(ln2_b))


def kernel(x, meta_tokens, in_ln_g, in_ln_b, rel_bias, w_in, b_in, attn_sinks, w_attn_proj, conv_dw, conv_dw_b, conv_ln_g, conv_ln_b, w_conv_proj, w_out, ln1_g, ln1_b, ffn_w_up, ffn_dw, ffn_dw_b, ffn_w_down, ln2_g, ln2_b):
    bband, bmeta, bmetaq = _bias_tables(rel_bias)
    ilg = in_ln_g[None, :].astype(F32)
    ilb = in_ln_b[None, :].astype(F32)
    h = x.astype(F32)
    hm = meta_tokens.astype(F32)
    for i in range(DEPTH):
        lw = _layer_weights(i, w_in, b_in, attn_sinks, w_attn_proj, conv_dw, conv_dw_b,
                            conv_ln_g, conv_ln_b, w_conv_proj, w_out, ln1_g, ln1_b, ffn_w_up,
                            ffn_dw, ffn_dw_b, ffn_w_down, ln2_g, ln2_b)
        hm_next, kvmeta, cmeta, upmeta = _meta_layer(i == 0, hm, ilg, ilb, lw, bmetaq)
        h = _main_layer(i == 0, h, ilg, ilb, lw, bband, bmeta, kvmeta, cmeta, upmeta)
        hm = hm_next
    return h
```
